```python
import jax, jax.numpy as jnp
from jax import lax
import numpy as np

D_MODEL = 1024
BATCH = 16
SEQ = 2048
DEPTH = 1

CHUNK = 64
NORM_EPS = 1e-6
SSM_D_INNER = 2 * D_MODEL
SSM_HEAD_DIM = 64
SSM_HEADS = SSM_D_INNER // SSM_HEAD_DIM
SSM_GROUPS = 4
SSM_STATE = 128
SSM_CONV = 4
SSM_XBC = SSM_D_INNER + 2 * SSM_GROUPS * SSM_STATE
ATT_HEADS = 16
ATT_HEAD_DIM = 64
ATT_WIDTH = ATT_HEADS * ATT_HEAD_DIM
KV_LATENT = 256
IDX_HEADS = 8
IDX_HEAD_DIM = 64
TOPK_MAX = 256
Q_BLOCK = 128
MOE_GROUPS = 4
MOE_EXPERTS = 8
MOE_TOPK = 2
MOE_FF = 256
N_BRANCHES = 2
IN_SIZES = (SSM_D_INNER, SSM_XBC, SSM_HEADS, ATT_WIDTH, KV_LATENT,
            IDX_HEADS * IDX_HEAD_DIM, IDX_HEAD_DIM, IDX_HEADS, N_BRANCHES * D_MODEL)
IN_TOTAL = 9064

kernel_name = "hybrid_ssd_dsa_hmoe_block"


def rms_norm(x, g):
    xf = x.astype(jnp.float32)
    y = xf * lax.rsqrt(jnp.mean(xf * xf, axis=-1, keepdims=True) + NORM_EPS)
    return (y * g.astype(jnp.float32)).astype(x.dtype)


def split_cols(u, sizes):
    cuts = [int(c) for c in np.cumsum(sizes)[:-1]]
    return jnp.split(u, cuts, axis=-1)


def causal_depthwise_conv(u, w, b):
    out = lax.conv_general_dilated(
        u, w[:, None, :].astype(u.dtype), window_strides=(1,),
        padding=[(SSM_CONV - 1, 0)], dimension_numbers=('NWC', 'WIO', 'NWC'),
        feature_group_count=u.shape[-1])
    return out + b.astype(u.dtype)


def ssd_chunked(xh, dt, a, bm, cm):
    b, s, h, p = xh.shape
    g, n = bm.shape[2], bm.shape[3]
    nc, hg = s // CHUNK, h // g
    x_c = xh.reshape(b, nc, CHUNK, g, hg, p)
    dt_c = dt.reshape(b, nc, CHUNK, g, hg)
    b_c = bm.reshape(b, nc, CHUNK, g, n)
    c_c = cm.reshape(b, nc, CHUNK, g, n)
    a_cs = jnp.cumsum(dt_c * a.reshape(g, hg), axis=2)
    seg = a_cs[:, :, :, None] - a_cs[:, :, None, :]
    causal = jnp.tril(jnp.ones((CHUNK, CHUNK), dtype=bool))[:, :, None, None]
    decay = jnp.exp(jnp.where(causal, seg, -jnp.inf))
    cb = jnp.einsum('bclgn,bcsgn->bclsg', c_c, b_c)
    wts = cb[..., None] * decay * dt_c[:, :, None]
    y_diag = jnp.einsum('bclsgh,bcsghp->bclghp', wts, x_c)
    decay_st = jnp.exp(a_cs[:, :, -1:] - a_cs) * dt_c
    states = jnp.einsum('bclgn,bclghp->bcghpn', b_c, decay_st[..., None] * x_c)
    chunk_decay = jnp.exp(a_cs[:, :, -1])

    def step(carry, inp):
        st, dec = inp
        return carry * dec[..., None, None] + st, carry

    init = jnp.zeros((b, g, hg, p, n), jnp.float32)
    _, prev = lax.scan(step, init, (jnp.moveaxis(states, 1, 0), jnp.moveaxis(chunk_decay, 1, 0)))
    prev = jnp.moveaxis(prev, 0, 1)
    y_off = jnp.einsum('bclgn,bcghpn->bclghp', c_c, prev) * jnp.exp(a_cs)[..., None]
    return (y_diag + y_off).reshape(b, s, h, p)


def mamba2_branch(z, xbc, dt_raw, conv_w, conv_b, dt_bias, a_log, d_skip, norm_g):
    b, s, _ = z.shape
    xbc = jax.nn.silu(causal_depthwise_conv(xbc, conv_w, conv_b))
    xs, bm, cm = split_cols(xbc, (SSM_D_INNER, SSM_GROUPS * SSM_STATE, SSM_GROUPS * SSM_STATE))
    xh = xs.reshape(b, s, SSM_HEADS, SSM_HEAD_DIM).astype(jnp.float32)
    bm = bm.reshape(b, s, SSM_GROUPS, SSM_STATE).astype(jnp.float32)
    cm = cm.reshape(b, s, SSM_GROUPS, SSM_STATE).astype(jnp.float32)
    dt = jax.nn.softplus(dt_raw.astype(jnp.float32) + dt_bias.astype(jnp.float32))
    a = -jnp.exp(a_log.astype(jnp.float32))
    y = ssd_chunked(xh, dt, a, bm, cm) + d_skip.astype(jnp.float32)[:, None] * xh
    y = y.reshape(b, s, SSM_D_INNER) * jax.nn.silu(z.astype(jnp.float32))
    yg = y.reshape(b, s, SSM_GROUPS, SSM_D_INNER // SSM_GROUPS)
    yg = yg * lax.rsqrt(jnp.mean(yg * yg, axis=-1, keepdims=True) + NORM_EPS)
    y = yg.reshape(b, s, SSM_D_INNER) * norm_g.astype(jnp.float32)
    return y.astype(z.dtype)


def dsa_branch(q, c_kv, q_idx, k_idx, w_idx, w_uk, w_uv):
    b, s = q.shape[0], q.shape[1]
    topk = min(TOPK_MAX, s // 4)
    q_abs = jnp.einsum('bshd,hrd->bshr', q, w_uk)
    slopes = 2.0 ** (-8.0 * jnp.arange(1, ATT_HEADS + 1, dtype=jnp.float32) / ATT_HEADS)
    key_pos = jnp.arange(s)

    def block(i):
        t0 = i * Q_BLOCK
        qa = lax.dynamic_slice_in_dim(q_abs, t0, Q_BLOCK, axis=1)
        qi = lax.dynamic_slice_in_dim(q_idx, t0, Q_BLOCK, axis=1)
        wi = lax.dynamic_slice_in_dim(w_idx, t0, Q_BLOCK, axis=1)
        t_pos = t0 + jnp.arange(Q_BLOCK)
        logit = jnp.einsum('bthd,bsd->bths', qi, k_idx).astype(jnp.float32) * (IDX_HEAD_DIM ** -0.5)
        score = jnp.einsum('bths,bth->bts', jax.nn.relu(logit), wi.astype(jnp.float32))
        visible = key_pos[None, :] < (t_pos[:, None] // CHUNK + 1) * CHUNK
        score = jnp.where(visible[None], score, -jnp.inf)
        top_val, top_idx = lax.top_k(score, topk)
        valid = top_val > -jnp.inf
        c_sel = jax.vmap(lambda c, ix: c[ix])(c_kv, top_idx)
        att = jnp.einsum('bthr,btkr->bthk', qa, c_sel).astype(jnp.float32) * (ATT_HEAD_DIM ** -0.5)
        dist = jnp.abs(t_pos[None, :, None] - top_idx).astype(jnp.float32)
        att = att - slopes[None, None, :, None] * dist[:, :, None, :]
        att = jnp.where(valid[:, :, None, :], att, -jnp.inf)
        p = jax.nn.softmax(att, axis=-1).astype(c_sel.dtype)
        o_lat = jnp.einsum('bthk,btkr->bthr', p, c_sel)
        return jnp.einsum('bthr,hrd->bthd', o_lat, w_uv)

    out = lax.map(block, jnp.arange(s // Q_BLOCK))
    return jnp.moveaxis(out, 0, 1).reshape(b, s, ATT_WIDTH)


def hier_moe(h, w_group, b_group, w_router, b_router, w_e1, w_e3, w_e2):
    b, s, d = h.shape
    t = h.reshape(b * s, d)
    g_logits = (t @ w_group + b_group).astype(jnp.float32)
    g_prob = jax.nn.softmax(g_logits, axis=-1)
    g_val, g_sel = lax.top_k(g_prob, 1)
    e_logits = (jnp.einsum('td,gde->tge', t, w_router) + b_router).astype(jnp.float32)
    e_logits = jnp.take_along_axis(e_logits, g_sel[:, :, None], axis=1)[:, 0]
    e_prob = jax.nn.softmax(e_logits, axis=-1)
    e_val, e_sel = lax.top_k(e_prob, MOE_TOPK)
    e_val = e_val / jnp.sum(e_val, axis=-1, keepdims=True)
    comb = jnp.sum(jax.nn.one_hot(e_sel, MOE_EXPERTS, dtype=jnp.float32) * e_val[..., None], axis=1)
    gate = jax.nn.one_hot(g_sel[:, 0], MOE_GROUPS, dtype=jnp.float32)[:, :, None] * (g_val * comb)[:, None, :]
    gate = gate.astype(h.dtype)
    y = jnp.zeros_like(t)
    for gi in range(MOE_GROUPS):
        hid = jax.nn.silu(jnp.einsum('td,edf->tef', t, w_e1[gi])) * jnp.einsum('td,edf->tef', t, w_e3[gi])
        y = y + jnp.einsum('tef,efd->td', hid * gate[:, gi, :, None], w_e2[gi])
    return y.reshape(b, s, d)


def setup_inputs(seed: int = 0) -> dict:
    key = jax.random.key(seed)
    ks = jax.random.split(key, 32)
    f32 = jnp.float32
    nrm = lambda k, shape, scale: jax.random.normal(k, shape, f32) * scale
    gain = lambda k, shape: 1.0 + 0.02 * jax.random.normal(k, shape, f32)
    dt0 = jnp.exp(jax.random.uniform(ks[5], (DEPTH, SSM_HEADS), f32, np.log(1e-3), np.log(1e-1)))
    return {
        "x": jax.random.normal(ks[0], (BATCH, SEQ, D_MODEL), f32),
        "norm_mix_g": gain(ks[1], (DEPTH, D_MODEL)),
        "w_in": nrm(ks[2], (DEPTH, D_MODEL, IN_TOTAL), D_MODEL ** -0.5),
        "conv_w": nrm(ks[3], (DEPTH, SSM_CONV, SSM_XBC), SSM_CONV ** -0.5),
        "conv_b": nrm(ks[4], (DEPTH, SSM_XBC), 0.01),
        "dt_bias": dt0 + jnp.log(-jnp.expm1(-dt0)),
        "a_log": jnp.log(jax.random.uniform(ks[6], (DEPTH, SSM_HEADS), f32, 1.0, 16.0)),
        "d_skip": 1.0 + 0.1 * jax.random.normal(ks[7], (DEPTH, SSM_HEADS), f32),
        "ssm_norm_g": gain(ks[8], (DEPTH, SSM_D_INNER)),
        "kv_norm_g": gain(ks[9], (DEPTH, KV_LATENT)),
        "w_uk": nrm(ks[10], (DEPTH, ATT_HEADS, KV_LATENT, ATT_HEAD_DIM), KV_LATENT ** -0.5),
        "w_uv": nrm(ks[11], (DEPTH, ATT_HEADS, KV_LATENT, ATT_HEAD_DIM), KV_LATENT ** -0.5),
        "gate_b": nrm(ks[12], (DEPTH, N_BRANCHES, D_MODEL), 0.01),
        "w_branch_ssm": nrm(ks[13], (DEPTH, SSM_D_INNER, D_MODEL), SSM_D_INNER ** -0.5),
        "w_branch_att": nrm(ks[14], (DEPTH, ATT_WIDTH, D_MODEL), ATT_WIDTH ** -0.5),
        "w_out": nrm(ks[15], (DEPTH, D_MODEL, D_MODEL), D_MODEL ** -0.5),
        "norm_ffn_g": gain(ks[16], (DEPTH, D_MODEL)),
        "w_group": nrm(ks[17], (DEPTH, D_MODEL, MOE_GROUPS), D_MODEL ** -0.5),
        "b_group": nrm(ks[18], (DEPTH, MOE_GROUPS), 0.01),
        "w_router": nrm(ks[19], (DEPTH, MOE_GROUPS, D_MODEL, MOE_EXPERTS), D_MODEL ** -0.5),
        "b_router": nrm(ks[20], (DEPTH, MOE_GROUPS, MOE_EXPERTS), 0.01),
        "w_e1": nrm(ks[21], (DEPTH, MOE_GROUPS, MOE_EXPERTS, D_MODEL, MOE_FF), D_MODEL ** -0.5),
        "w_e3": nrm(ks[22], (DEPTH, MOE_GROUPS, MOE_EXPERTS, D_MODEL, MOE_FF), D_MODEL ** -0.5),
        "w_e2": nrm(ks[23], (DEPTH, MOE_GROUPS, MOE_EXPERTS, MOE_FF, D_MODEL), MOE_FF ** -0.5),
        "norm_final_g": gain(ks[24], (D_MODEL,)),
    }


def reference(x, norm_mix_g, w_in, conv_w, conv_b, dt_bias, a_log, d_skip, ssm_norm_g,
              kv_norm_g, w_uk, w_uv, gate_b, w_branch_ssm, w_branch_att, w_out,
              norm_ffn_g, w_group, b_group, w_router, b_router, w_e1, w_e3, w_e2,
              norm_final_g):
    b, s, d = x.shape
    for layer in range(DEPTH):
        h = rms_norm(x, norm_mix_g[layer])
        u = h @ w_in[layer]
        z, xbc, dt_raw, q, c_kv, q_idx, k_idx, w_idx, gates = split_cols(u, IN_SIZES)
        y_ssm = mamba2_branch(z, xbc, dt_raw, conv_w[layer], conv_b[layer], dt_bias[layer],
                              a_log[layer], d_skip[layer], ssm_norm_g[layer])
        c_kv = rms_norm(c_kv, kv_norm_g[layer])
        w_idx = w_idx * (IDX_HEADS ** -0.5)
        y_att = dsa_branch(q.reshape(b, s, ATT_HEADS, ATT_HEAD_DIM), c_kv,
                           q_idx.reshape(b, s, IDX_HEADS, IDX_HEAD_DIM), k_idx, w_idx,
                           w_uk[layer], w_uv[layer])
        g = jax.nn.sigmoid(gates.reshape(b, s, N_BRANCHES, d) + gate_b[layer])
        mixed = g[:, :, 0] * (y_ssm @ w_branch_ssm[layer]) + g[:, :, 1] * (y_att @ w_branch_att[layer])
        x = x + mixed @ w_out[layer]
        h = rms_norm(x, norm_ffn_g[layer])
        x = x + hier_moe(h, w_group[layer], b_group[layer], w_router[layer], b_router[layer],
                         w_e1[layer], w_e3[layer], w_e2[layer])
    return rms_norm(x, norm_final_g)
```

```python
import functools

import numpy as np
import jax
import jax.numpy as jnp
from jax import lax
from jax.experimental import pallas as pl
from jax.experimental.pallas import tpu as pltpu

F32 = jnp.float32
BF16 = jnp.bfloat16

NORM_EPS = 1e-6
CHUNK = 64
SSM_HEAD_DIM = 64
SSM_GROUPS = 4
SSM_STATE = 128
SSM_CONV = 4
ATT_HEADS = 16
ATT_HEAD_DIM = 64
KV_LATENT = 256
IDX_HEADS = 8
IDX_HEAD_DIM = 64
TOPK_MAX = 256
MOE_GROUPS = 4
MOE_EXPERTS = 8
MOE_FF = 256

LANES = 128
Q_TILE = 128
CONV_PAD = 8
VMEM_LIMIT = 56 * 1024 * 1024
INT_MIN = -(2 ** 31)

COL_Z, COL_XS, COL_BC, COL_Q, COL_QI, COL_CKV, COL_G = 0, 2048, 4096, 5120, 6144, 6656, 7168
N_BIG = 9216
N_SMALL = 256
SM_DT, SM_W = 0, 32


def _nt_dot(a, b):
    return lax.dot_general(a, b, (((1,), (1,)), ((), ())), preferred_element_type=F32)


def _dot(a, b):
    return jnp.dot(a, b, preferred_element_type=F32)


def _split3(a):
    hi = a.astype(BF16)
    r = a - hi.astype(F32)
    mid = r.astype(BF16)
    lo = (r - mid.astype(F32)).astype(BF16)
    return hi, mid, lo


def _silu(v):
    return v * (1.0 / (1.0 + jnp.exp(-v)))


def _in_proj_kernel(x_ref, g_ref, wb_ref, ws_ref, ub_ref, us_ref, h_ref):
    j = pl.program_id(1)

    @pl.when(j == 0)
    def _():
        xf = x_ref[...]
        y = xf * lax.rsqrt(jnp.mean(xf * xf, axis=-1, keepdims=True) + NORM_EPS)
        h = (y * g_ref[...]).astype(BF16)
        h_ref[...] = h
        us_ref[...] = _dot(h, ws_ref[...])

    ub_ref[...] = _dot(h_ref[...], wb_ref[...]).astype(BF16)


def _in_proj(x2, g, w_big, w_small, tm, tn):
    t, d = x2.shape
    return pl.pallas_call(
        _in_proj_kernel,
        grid=(t // tm, N_BIG // tn),
        in_specs=[
            pl.BlockSpec((tm, d), lambda i, j: (i, 0)),
            pl.BlockSpec((1, d), lambda i, j: (0, 0)),
            pl.BlockSpec((d, tn), lambda i, j: (0, j)),
            pl.BlockSpec((d, N_SMALL), lambda i, j: (0, 0)),
        ],
        out_specs=[
            pl.BlockSpec((tm, tn), lambda i, j: (i, j)),
            pl.BlockSpec((tm, N_SMALL), lambda i, j: (i, 0)),
        ],
        out_shape=[
            jax.ShapeDtypeStruct((t, N_BIG), BF16),
            jax.ShapeDtypeStruct((t, N_SMALL), F32),
        ],
        scratch_shapes=[pltpu.VMEM((tm, d), BF16)],
        compiler_params=pltpu.CompilerParams(
            dimension_semantics=("arbitrary", "arbitrary"), vmem_limit_bytes=VMEM_LIMIT),
        name="in_proj",
    )(x2, g, w_big, w_small)


def _ssd_kernel(z_ref, xs_ref, bc_ref, sm_ref, cwx_ref, cwb_ref, cbx_ref, cbb_ref,
                dtb_ref, alog_ref, dsk_ref, ng_ref, e3_ref, tri_ref, it_ref, cm_ref, bm_ref,
                y_ref, xpx_ref, xpb_ref, st_ref, *, nch):
    lb = nch * CHUNK
    d_in = xs_ref.shape[-1]
    heads = d_in // SSM_HEAD_DIM
    gw = d_in // SSM_GROUPS
    gn = SSM_STATE

    @pl.when(pl.program_id(1) == 0)
    def _():
        xpx_ref[0:CONV_PAD, :] = jnp.zeros((CONV_PAD, d_in), F32)
        xpb_ref[0:CONV_PAD, :] = jnp.zeros((CONV_PAD, xpb_ref.shape[-1]), F32)
        st_ref[...] = jnp.zeros(st_ref.shape, F32)

    def conv(src_ref, pad_ref, w_ref, b_ref):
        pad_ref[CONV_PAD:CONV_PAD + lb, :] = src_ref[...].astype(F32)
        acc = b_ref[...] + w_ref[0:1, :] * pad_ref[CONV_PAD - 3:CONV_PAD - 3 + lb, :]
        for k in range(1, SSM_CONV):
            off = CONV_PAD - (SSM_CONV - 1) + k
            acc = acc + w_ref[k:k + 1, :] * pad_ref[off:off + lb, :]
        pad_ref[0:CONV_PAD, :] = pad_ref[lb:lb + CONV_PAD, :]
        return _silu(acc)

    xs = conv(xs_ref, xpx_ref, cwx_ref, cbx_ref)
    bc = conv(bc_ref, xpb_ref, cwb_ref, cbb_ref)

    dt_in = sm_ref[:, SM_DT:SM_DT + heads] + dtb_ref[...]
    dt = jnp.maximum(dt_in, 0.0) + jnp.log1p(jnp.exp(-jnp.abs(dt_in)))
    da = dt * (-jnp.exp(alog_ref[...]))
    tri = tri_ref[...]
    hi, mid, lo = _split3(da)
    a_cs = _dot(tri, hi) + _dot(tri, mid) + _dot(tri, lo)
    a_last = jnp.concatenate(
        [jnp.broadcast_to(a_cs[c * CHUNK + CHUNK - 1:c * CHUNK + CHUNK, :], (CHUNK, heads))
         for c in range(nch)], axis=0)
    stacked = jnp.concatenate([a_cs, jnp.exp(a_cs), jnp.exp(a_last - a_cs), dt], axis=0)
    s_hi, s_mid, s_lo = _split3(stacked)
    ex = _dot(jnp.concatenate([s_hi, s_mid, s_lo], axis=1), e3_ref[...])

    it_mask = it_ref[...] > 0.0
    cm_mask = cm_ref[...] > 0.0
    bm = bm_ref[...]
    it_bf = it_ref[...].astype(BF16)
    zf = z_ref[...].astype(F32)
    zg = _silu(zf)

    for c in range(nch):
        r0 = c * CHUNK
        a_l = ex[r0:r0 + CHUNK]
        exp_l = ex[lb + r0:lb + r0 + CHUNK]
        dst = ex[2 * lb + r0:2 * lb + r0 + CHUNK]
        dtx = ex[3 * lb + r0:3 * lb + r0 + CHUNK]
        xs_c = xs[r0:r0 + CHUNK]
        xdt = xs_c * dtx
        a_s = jnp.sum(jnp.where(it_mask, a_l, 0.0), axis=0, keepdims=True)
        decay = jnp.exp(jnp.where(cm_mask, a_l - a_s, -jnp.inf))
        xst = (xdt * dst).astype(BF16)
        xdt_bf = xdt.astype(BF16)
        y_parts = []
        for g in range(SSM_GROUPS):
            b_g = bc[r0:r0 + CHUNK, g * gn:(g + 1) * gn]
            c_g = bc[r0:r0 + CHUNK, (SSM_GROUPS + g) * gn:(SSM_GROUPS + g + 1) * gn]
            b_bf = b_g.astype(BF16)
            c_bf = c_g.astype(BF16)
            cb = _nt_dot(c_bf, b_bf)
            cbx = _dot(cb.astype(BF16), it_bf[:, 0:gw])
            w = (cbx * decay[:, g * gw:(g + 1) * gw]).astype(BF16)
            yd = []
            for jt in range(gw // 256):
                lo_c = g * gw + jt * 256
                xt = xdt_bf[:, lo_c:lo_c + 256]
                bd = jnp.concatenate([xt] * 4, axis=0) * bm
                yd.append(_dot(w[:, jt * 256:(jt + 1) * 256], bd))
            y_diag = jnp.concatenate(yd, axis=1)
            st_g = st_ref[g]
            y_off = _dot(c_bf, st_g.astype(BF16)) * exp_l[:, g * gw:(g + 1) * gw]
            new_st = _dot(jnp.transpose(b_g).astype(BF16), xst[:, g * gw:(g + 1) * gw])
            st_ref[g] = st_g * exp_l[CHUNK - 1:CHUNK, g * gw:(g + 1) * gw] + new_st
            y_g = y_diag + y_off + dsk_ref[:, g * gw:(g + 1) * gw] * xs_c[:, g * gw:(g + 1) * gw]
            y_g = y_g * zg[r0:r0 + CHUNK, g * gw:(g + 1) * gw]
            y_g = y_g * lax.rsqrt(jnp.mean(y_g * y_g, axis=-1, keepdims=True) + NORM_EPS)
            y_parts.append(y_g * ng_ref[:, g * gw:(g + 1) * gw])
        y_ref[r0:r0 + CHUNK, :] = jnp.concatenate(y_parts, axis=1).astype(BF16)


def _ssd(u_big, u_small, conv_w, conv_b, dt_bias, a_log, d_skip, ssm_norm_g, b, s, nch):
    t = b * s
    d_in = COL_BC - COL_XS
    n_bc = COL_Q - COL_BC
    heads = d_in // SSM_HEAD_DIM
    lb = nch * CHUNK
    nblk = s // lb
    assert d_in // SSM_GROUPS == 512 and SSM_HEAD_DIM == CHUNK

    e = np.kron(np.eye(heads, dtype=np.float32), np.ones((1, SSM_HEAD_DIM), np.float32))
    e3 = jnp.asarray(np.concatenate([e, e, e], axis=0), BF16)
    tri = jnp.asarray(np.kron(np.eye(nch, dtype=np.float32), np.tril(np.ones((CHUNK, CHUNK), np.float32))), BF16)
    it = jnp.asarray(np.tile(np.eye(CHUNK, dtype=np.float32), (1, heads)))
    cm = jnp.asarray(np.tile(np.tril(np.ones((CHUNK, CHUNK), np.float32)), (1, heads)))
    bm = jnp.asarray(np.kron(np.eye(4, dtype=np.float32), np.ones((64, 64), np.float32)), BF16)

    cwx, cwb = conv_w[:, :d_in], conv_w[:, d_in:]
    cbx, cbb = conv_b[None, :d_in], conv_b[None, d_in:]
    dsk = jnp.repeat(d_skip, SSM_HEAD_DIM)[None, :]
    const = lambda shape: pl.BlockSpec(shape, lambda bi, ci: (0,) * len(shape))
    row = lambda width, col: pl.BlockSpec((lb, width), lambda bi, ci: (bi * nblk + ci, col))

    return pl.pallas_call(
        functools.partial(_ssd_kernel, nch=nch),
        grid=(b, nblk),
        in_specs=[
            row(d_in, COL_Z // d_in), row(d_in, COL_XS // d_in), row(n_bc, COL_BC // n_bc),
            row(LANES, 1),
            const((SSM_CONV, d_in)), const((SSM_CONV, n_bc)), const((1, d_in)), const((1, n_bc)),
            const((1, heads)), const((1, heads)), const((1, d_in)), const((1, d_in)),
            const((3 * heads, d_in)), const((lb, lb)), const((CHUNK, d_in)), const((CHUNK, d_in)),
            const((256, 256)),
        ],
        out_specs=pl.BlockSpec((lb, d_in), lambda bi, ci: (bi * nblk + ci, 0)),
        out_shape=jax.ShapeDtypeStruct((t, d_in), BF16),
        scratch_shapes=[
            pltpu.VMEM((lb + CONV_PAD, d_in), F32),
            pltpu.VMEM((lb + CONV_PAD, n_bc), F32),
            pltpu.VMEM((SSM_GROUPS, SSM_STATE, d_in // SSM_GROUPS), F32),
        ],
        compiler_params=pltpu.CompilerParams(
            dimension_semantics=("arbitrary", "arbitrary"), vmem_limit_bytes=VMEM_LIMIT),
        name="ssd",
    )(u_big, u_big, u_big, u_small, cwx, cwb, cbx, cbb, dt_bias[None, :], a_log[None, :], dsk,
      ssm_norm_g[None, :], e3, tri, it, cm, bm)


def _kv_proj_kernel(c_ref, g_ref, wk_ref, wvt_ref, k_ref, vt_ref):
    cf = c_ref[...].astype(F32)
    y = cf * lax.rsqrt(jnp.mean(cf * cf, axis=-1, keepdims=True) + NORM_EPS)
    cn = (y * g_ref[...]).astype(BF16)
    k_ref[0] = _dot(cn, wk_ref[...]).astype(BF16)
    vt_ref[0] = _nt_dot(wvt_ref[...], cn).astype(BF16)


def _kv_proj(u_big, kv_norm_g, w_k, w_vt, b, s):
    width = ATT_HEADS * ATT_HEAD_DIM
    return pl.pallas_call(
        _kv_proj_kernel,
        grid=(b,),
        in_specs=[
            pl.BlockSpec((s, KV_LATENT), lambda bi: (bi, COL_CKV // KV_LATENT)),
            pl.BlockSpec((1, KV_LATENT), lambda bi: (0, 0)),
            pl.BlockSpec((KV_LATENT, width), lambda bi: (0, 0)),
            pl.BlockSpec((width, KV_LATENT), lambda bi: (0, 0)),
        ],
        out_specs=[
            pl.BlockSpec((1, s, width), lambda bi: (bi, 0, 0)),
            pl.BlockSpec((1, width, s), lambda bi: (bi, 0, 0)),
        ],
        out_shape=[
            jax.ShapeDtypeStruct((b, s, width), BF16),
            jax.ShapeDtypeStruct((b, width, s), BF16),
        ],
        compiler_params=pltpu.CompilerParams(
            dimension_semantics=("arbitrary",), vmem_limit_bytes=VMEM_LIMIT),
        name="kv_proj",
    )(u_big, kv_norm_g, w_k, w_vt)


def _dsa_kernel(q_ref, qi_ref, smq_ref, smk_ref, k_ref, vt_ref, o_ref,
                kidx_ref, sc_ref, dist_ref, mb_ref, jcut_ref, ot_ref, *, buckets, topk, slopes):
    i = pl.program_id(1)

    @pl.when(i == 0)
    def _():
        kidx_ref[...] = smk_ref[...].astype(BF16)

    lane = lax.broadcasted_iota(jnp.int32, (Q_TILE, LANES), 1)
    low_half = lane < IDX_HEAD_DIM
    w_t = jnp.transpose(smq_ref[...])[SM_W:SM_W + IDX_HEADS, :] * (IDX_HEADS ** -0.5 * IDX_HEAD_DIM ** -0.5)
    kf = float(topk)

    def body(kv_len):
        n = kv_len
        s_pos = lax.broadcasted_iota(jnp.int32, (n, Q_TILE), 0)
        t_pos = i * Q_TILE + lax.broadcasted_iota(jnp.int32, (n, Q_TILE), 1)
        visible = s_pos < (t_pos // CHUNK + 1) * CHUNK

        score = jnp.zeros((n, Q_TILE), F32)
        for h in range(IDX_HEADS):
            pair, half = divmod(h, 2)
            qi = qi_ref[:, pair * LANES:(pair + 1) * LANES]
            qi = jnp.where(low_half if half == 0 else jnp.logical_not(low_half), qi, jnp.zeros_like(qi))
            logit = _nt_dot(kidx_ref[0:n, :], qi)
            score = score + jnp.maximum(logit, 0.0) * w_t[h:h + 1, :]

        sc_ref[0:n, :] = jnp.where(visible, score, -jnp.inf)
        dist_ref[0:n, :] = jnp.abs(t_pos - s_pos).astype(F32)
        t_row = i * Q_TILE + lax.broadcasted_iota(jnp.int32, (1, Q_TILE), 1)
        many = jnp.minimum((t_row // CHUNK + 1) * CHUNK, n) > topk

        def count(pred):
            return jnp.sum(jnp.where(pred, 1.0, 0.0), axis=0, keepdims=True)

        def bit_step(it, u):
            cand = u | (jnp.int32(1) << (31 - it))
            cand_f = pltpu.bitcast(jnp.where(cand < 0, cand ^ INT_MIN, ~cand), F32)
            c = count(sc_ref[0:n, :] >= cand_f)
            return jnp.where(c >= kf, cand, u)

        u_thr = lax.fori_loop(0, 32, bit_step, jnp.zeros((1, Q_TILE), jnp.int32))
        thr = pltpu.bitcast(jnp.where(u_thr < 0, u_thr ^ INT_MIN, ~u_thr), F32)
        thr = jnp.where(many, thr, -jnp.inf)
        sc = sc_ref[0:n, :]
        gt = sc > thr
        eq = sc == thr
        need = kf - count(gt)
        excess = jnp.where(many, count(eq) - need, 0.0)
        jcut_ref[...] = jnp.where(many, n, -1)

        @pl.when(jnp.max(excess) > 0.0)
        def _():
            nbits = max(1, int(np.ceil(np.log2(n))))

            def idx_step(it, j0):
                cand = j0 | (jnp.int32(1) << (nbits - 1 - it))
                c = count((sc_ref[0:n, :] == thr) & (s_pos < cand))
                return jnp.where(c < need, cand, j0)

            j0 = lax.fori_loop(0, nbits, idx_step, jnp.zeros((1, Q_TILE), jnp.int32))
            jcut_ref[...] = jnp.where(many, j0, -1)

        sel = gt | (eq & (s_pos <= jcut_ref[...]))
        mb_ref[0:n, :] = jnp.where(sel, 0.0, -jnp.inf)

        for h in range(ATT_HEADS):
            pair, half = divmod(h, 2)
            qh = q_ref[:, pair * LANES:(pair + 1) * LANES]
            qh = jnp.where(low_half if half == 0 else jnp.logical_not(low_half), qh, jnp.zeros_like(qh))
            qh = qh * jnp.asarray(ATT_HEAD_DIM ** -0.5, BF16)
            att = _nt_dot(k_ref[0, 0:n, pair * LANES:(pair + 1) * LANES], qh)
            att = att - slopes[h] * dist_ref[0:n, :] + mb_ref[0:n, :]
            m = jnp.max(att, axis=0, keepdims=True)
            p = jnp.exp(att - m)
            den = jnp.sum(p, axis=0, keepdims=True)
            o_t = _dot(vt_ref[0, h * ATT_HEAD_DIM:(h + 1) * ATT_HEAD_DIM, 0:n], p.astype(BF16))
            ot_ref[h * ATT_HEAD_DIM:(h + 1) * ATT_HEAD_DIM, :] = o_t * (1.0 / den)
        o_ref[...] = jnp.transpose(ot_ref[...]).astype(BF16)

    need_len = (i + 1) * Q_TILE
    lo_b = 0
    for kv_len in buckets:
        pl.when((need_len > lo_b) & (need_len <= kv_len))(functools.partial(body, kv_len))
        lo_b = kv_len


def _dsa(u_big, u_small, k_all, vt_all, b, s):
    t = b * s
    width = ATT_HEADS * ATT_HEAD_DIM
    nq = s // Q_TILE
    topk = min(TOPK_MAX, s // 4)
    buckets = tuple(sorted({min(s, v) for v in (256, 512, 1024, 1536, 2048) if v < s} | {s}))
    slopes = tuple(float(2.0 ** (-8.0 * (h + 1) / ATT_HEADS)) for h in range(ATT_HEADS))
    qi_w = IDX_HEADS * IDX_HEAD_DIM
    return pl.pallas_call(
        functools.partial(_dsa_kernel, buckets=buckets, topk=topk, slopes=slopes),
        grid=(b, nq),
        in_specs=[
            pl.BlockSpec((Q_TILE, width), lambda bi, qi: (bi * nq + qi, COL_Q // width)),
            pl.BlockSpec((Q_TILE, qi_w), lambda bi, qi: (bi * nq + qi, COL_QI // qi_w)),
            pl.BlockSpec((Q_TILE, LANES), lambda bi, qi: (bi * nq + qi, 1)),
            pl.BlockSpec((s, LANES), lambda bi, qi: (bi, 0)),
            pl.BlockSpec((1, s, width), lambda bi, qi: (bi, 0, 0)),
            pl.BlockSpec((1, width, s), lambda bi, qi: (bi, 0, 0)),
        ],
        out_specs=pl.BlockSpec((Q_TILE, width), lambda bi, qi: (bi * nq + qi, 0)),
        out_shape=jax.ShapeDtypeStruct((t, width), BF16),
        scratch_shapes=[
            pltpu.VMEM((s, LANES), BF16),
            pltpu.VMEM((s, Q_TILE), F32),
            pltpu.VMEM((s, Q_TILE), F32),
            pltpu.VMEM((s, Q_TILE), F32),
            pltpu.VMEM((1, Q_TILE), jnp.int32),
            pltpu.VMEM((width, Q_TILE), F32),
        ],
        compiler_params=pltpu.CompilerParams(
            dimension_semantics=("arbitrary", "arbitrary"), vmem_limit_bytes=VMEM_LIMIT),
        name="dsa",
    )(u_big, u_big, u_small, u_small, k_all, vt_all)


def _merge_kernel(x_ref, ys_ref, ya_ref, g0_ref, g1_ref, gb_ref, wbs_ref, wba_ref, wo_ref,
                  nf_ref, wr_ref, br_ref, x1_ref, h2_ref, gate_ref):
    def sig(v):
        return 1.0 / (1.0 + jnp.exp(-v))

    g0 = sig(g0_ref[...].astype(F32) + gb_ref[0:1, :])
    g1 = sig(g1_ref[...].astype(F32) + gb_ref[1:2, :])
    mixed = g0 * _dot(ys_ref[...], wbs_ref[...]) + g1 * _dot(ya_ref[...], wba_ref[...])
    x1 = x_ref[...] + _dot(mixed.astype(BF16), wo_ref[...])
    x1_ref[...] = x1
    h2 = x1 * lax.rsqrt(jnp.mean(x1 * x1, axis=-1, keepdims=True) + NORM_EPS) * nf_ref[...]
    h2b = h2.astype(BF16)
    h2_ref[...] = h2b

    n_e = MOE_GROUPS * MOE_EXPERTS
    logits = _dot(h2b, wr_ref[...]) + br_ref[...]
    lane = lax.broadcasted_iota(jnp.int32, logits.shape, 1)
    neg = -jnp.inf
    is_g = (lane >= n_e) & (lane < n_e + MOE_GROUPS)
    gl = jnp.where(is_g, logits, neg)
    gmax = jnp.max(gl, axis=-1, keepdims=True)
    g_val = 1.0 / jnp.sum(jnp.exp(gl - gmax), axis=-1, keepdims=True)
    g_sel = jnp.min(jnp.where(is_g & (gl == gmax), lane, 4 * LANES), axis=-1, keepdims=True) - n_e
    in_grp = (lane < n_e) & ((lane // MOE_EXPERTS) == g_sel)
    el = jnp.where(in_grp, logits, neg)
    m1 = jnp.max(el, axis=-1, keepdims=True)
    i1 = jnp.min(jnp.where(in_grp & (el == m1), lane, 4 * LANES), axis=-1, keepdims=True)
    el2 = jnp.where(lane == i1, neg, el)
    m2 = jnp.max(el2, axis=-1, keepdims=True)
    i2 = jnp.min(jnp.where(in_grp & (el2 == m2), lane, 4 * LANES), axis=-1, keepdims=True)
    p2 = jnp.exp(m2 - m1)
    inv = 1.0 / (1.0 + p2)
    gate = jnp.where(lane == i1, inv, 0.0) + jnp.where(lane == i2, p2 * inv, 0.0)
    gate_ref[...] = gate * g_val


def _merge(x2, y_ssm, y_att, u_big, gate_b, w_bs, w_ba, w_o, norm_ffn_g, w_r, b_r, tm):
    t, d = x2.shape
    d_in = y_ssm.shape[1]
    const = lambda shape: pl.BlockSpec(shape, lambda i: (0,) * len(shape))
    return pl.pallas_call(
        _merge_kernel,
        grid=(t // tm,),
        in_specs=[
            pl.BlockSpec((tm, d), lambda i: (i, 0)),
            pl.BlockSpec((tm, d_in), lambda i: (i, 0)),
            pl.BlockSpec((tm, d), lambda i: (i, 0)),
            pl.BlockSpec((tm, d), lambda i: (i, COL_G // d)),
            pl.BlockSpec((tm, d), lambda i: (i, COL_G // d + 1)),
            const((2, d)), const((d_in, d)), const((d, d)), const((d, d)), const((1, d)),
            const((d, LANES)), const((1, LANES)),
        ],
        out_specs=[
            pl.BlockSpec((tm, d), lambda i: (i, 0)),
            pl.BlockSpec((tm, d), lambda i: (i, 0)),
            pl.BlockSpec((tm, LANES), lambda i: (i, 0)),
        ],
        out_shape=[
            jax.ShapeDtypeStruct((t, d), F32),
            jax.ShapeDtypeStruct((t, d), BF16),
            jax.ShapeDtypeStruct((t, LANES), F32),
        ],
        compiler_params=pltpu.CompilerParams(
            dimension_semantics=("arbitrary",), vmem_limit_bytes=VMEM_LIMIT),
        name="merge",
    )(x2, y_ssm, y_att, u_big, u_big, gate_b, w_bs, w_ba, w_o, norm_ffn_g, w_r, b_r)


def _moe_kernel(h_ref, gate_ref, x1_ref, w1_ref, w3_ref, w2_ref, nf_ref, o_ref, acc_ref):
    e = pl.program_id(1)

    @pl.when(e == 0)
    def _():
        acc_ref[...] = x1_ref[...]

    h = h_ref[...]
    a = _dot(h, w1_ref[0])
    bgate = _dot(h, w3_ref[0])
    lane = lax.broadcasted_iota(jnp.int32, gate_ref.shape, 1)
    g_col = jnp.sum(jnp.where(lane == e, gate_ref[...], 0.0), axis=-1, keepdims=True)
    hid = _silu(a) * bgate * g_col
    acc_ref[...] += _dot(hid.astype(BF16), w2_ref[0])

    @pl.when(e == pl.num_programs(1) - 1)
    def _():
        x2 = acc_ref[...]
        o_ref[...] = x2 * lax.rsqrt(jnp.mean(x2 * x2, axis=-1, keepdims=True) + NORM_EPS) * nf_ref[...]


def _moe(h2, gate, x1, w1, w3, w2, norm_final_g, tm):
    t, d = x1.shape
    n_e, _, ff = w1.shape
    return pl.pallas_call(
        _moe_kernel,
        grid=(t // tm, n_e),
        in_specs=[
            pl.BlockSpec((tm, d), lambda i, e: (i, 0)),
            pl.BlockSpec((tm, LANES), lambda i, e: (i, 0)),
            pl.BlockSpec((tm, d), lambda i, e: (i, 0)),
            pl.BlockSpec((1, d, ff), lambda i, e: (e, 0, 0)),
            pl.BlockSpec((1, d, ff), lambda i, e: (e, 0, 0)),
            pl.BlockSpec((1, ff, d), lambda i, e: (e, 0, 0)),
            pl.BlockSpec((1, d), lambda i, e: (0, 0)),
        ],
        out_specs=pl.BlockSpec((tm, d), lambda i, e: (i, 0)),
        out_shape=jax.ShapeDtypeStruct((t, d), F32),
        scratch_shapes=[pltpu.VMEM((tm, d), F32)],
        compiler_params=pltpu.CompilerParams(
            dimension_semantics=("arbitrary", "arbitrary"), vmem_limit_bytes=VMEM_LIMIT),
        name="moe",
    )(h2, gate, x1, w1, w3, w2, norm_final_g)


def _pick_tile(t, pref):
    tm = pref
    while t % tm:
        tm //= 2
    return tm


def kernel(x, norm_mix_g, w_in, conv_w, conv_b, dt_bias, a_log, d_skip, ssm_norm_g, kv_norm_g, w_uk, w_uv,
           gate_b, w_branch_ssm, w_branch_att, w_out, norm_ffn_g, w_group, b_group, w_router, b_router,
           w_e1, w_e3, w_e2, norm_final_g):
    b, s, d = x.shape
    depth = w_in.shape[0]
    t = b * s
    d_in = 2 * d
    heads = d_in // SSM_HEAD_DIM
    n_bc = 2 * SSM_GROUPS * SSM_STATE
    width = ATT_HEADS * ATT_HEAD_DIM
    qi_w = IDX_HEADS * IDX_HEAD_DIM
    assert (d, d_in, n_bc, width) == (1024, COL_XS - COL_Z, COL_Q - COL_BC, COL_QI - COL_Q)
    assert s % (2 * CHUNK) == 0 and s % Q_TILE == 0

    assert depth == 1, "the final RMSNorm is fused into the expert kernel of the only layer"
    layer = 0
    x2 = x.reshape(t, d)

    sizes = (d_in, d_in + n_bc, heads, width, KV_LATENT, qi_w, IDX_HEAD_DIM, IDX_HEADS, 2 * d)
    cuts = np.concatenate([[0], np.cumsum(sizes)])
    seg = lambda k: w_in[layer][:, int(cuts[k]):int(cuts[k + 1])]
    w_z, w_xbc, w_dt, w_q, w_ckv, w_qi, w_ki, w_wi, w_g = (seg(k) for k in range(9))
    zpad = lambda n: jnp.zeros((d, n), w_in.dtype)
    w_big = jnp.concatenate(
        [w_z, w_xbc, w_q, w_qi, w_ckv, zpad(COL_G - COL_CKV - KV_LATENT), w_g], axis=1).astype(BF16)
    w_small = jnp.concatenate(
        [w_ki, w_ki, w_dt, w_wi, zpad(N_SMALL - 2 * IDX_HEAD_DIM - heads - IDX_HEADS)], axis=1).astype(BF16)
    assert w_big.shape[1] == N_BIG and w_small.shape[1] == N_SMALL

    u_big, u_small = _in_proj(x2, norm_mix_g[layer][None, :], w_big, w_small, _pick_tile(t, 1024), 1536)

    y_ssm = _ssd(u_big, u_small, conv_w[layer], conv_b[layer], dt_bias[layer], a_log[layer],
                 d_skip[layer], ssm_norm_g[layer], b, s, 2)

    w_k = jnp.transpose(w_uk[layer], (1, 0, 2)).reshape(KV_LATENT, width).astype(BF16)
    w_vt = jnp.transpose(w_uv[layer], (0, 2, 1)).reshape(width, KV_LATENT).astype(BF16)
    k_all, vt_all = _kv_proj(u_big, kv_norm_g[layer][None, :], w_k, w_vt, b, s)
    y_att = _dsa(u_big, u_small, k_all, vt_all, b, s)

    n_e = MOE_GROUPS * MOE_EXPERTS
    w_r = jnp.concatenate(
        [jnp.transpose(w_router[layer], (1, 0, 2)).reshape(d, n_e), w_group[layer],
         jnp.zeros((d, LANES - n_e - MOE_GROUPS), F32)], axis=1).astype(BF16)
    b_r = jnp.concatenate(
        [b_router[layer].reshape(n_e), b_group[layer], jnp.zeros((LANES - n_e - MOE_GROUPS,), F32)])[None, :]
    x1, h2, gate = _merge(x2, y_ssm, y_att, u_big, gate_b[layer],
                          w_branch_ssm[layer].astype(BF16), w_branch_att[layer].astype(BF16),
                          w_out[layer].astype(BF16), norm_ffn_g[layer][None, :], w_r, b_r,
                          _pick_tile(t, 512))

    out = _moe(h2, gate, x1,
               w_e1[layer].reshape(n_e, d, MOE_FF).astype(BF16),
               w_e3[layer].reshape(n_e, d, MOE_FF).astype(BF16),
               w_e2[layer].reshape(n_e, MOE_FF, d).astype(BF16),
               norm_final_g[None, :], _pick_tile(t, 1024))
    return out.reshape(b, s, d)
```

```python
import functools

import numpy as np
import jax
import jax.numpy as jnp
from jax import lax
from jax.experimental import pallas as pl
from jax.experimental.pallas import tpu as pltpu

F32 = jnp.float32
BF16 = jnp.bfloat16

NORM_EPS = 1e-6
CHUNK = 64
SSM_HEAD_DIM = 64
SSM_GROUPS = 4
SSM_STATE = 128
SSM_CONV = 4
ATT_HEADS = 16
ATT_HEAD_DIM = 64
KV_LATENT = 256
IDX_HEADS = 8
IDX_HEAD_DIM = 64
TOPK_MAX = 256
MOE_GROUPS = 4
MOE_EXPERTS = 8
MOE_FF = 256

LANES = 128
Q_TILE = 128
CONV_PAD = 8
VMEM_LIMIT = 56 * 1024 * 1024
INT_MIN = -(2 ** 31)
LOG2E = 1.4426950408889634
ROW_SLAB = 64
KEY_CHUNK = 256
SOFTMAX_M0 = -1e30

COL_Z, COL_XS, COL_BC, COL_Q, COL_QI, COL_CKV, COL_G = 0, 2048, 4096, 5120, 6144, 6656, 7168
N_BIG = 9216
N_SMALL = 256
SM_DT, SM_W = 0, 32


def _nt_dot(a, b):
    return lax.dot_general(a, b, (((1,), (1,)), ((), ())), preferred_element_type=F32)


def _dot(a, b):
    return jnp.dot(a, b, preferred_element_type=F32)


def _split3(a):
    hi = a.astype(BF16)
    r = a - hi.astype(F32)
    mid = r.astype(BF16)
    lo = (r - mid.astype(F32)).astype(BF16)
    return hi, mid, lo


def _silu(v):
    return v * (1.0 / (1.0 + jnp.exp(-v)))


def _in_proj_kernel(x_ref, g_ref, wb_ref, ws_ref, ub_ref, us_ref, h_ref):
    j = pl.program_id(1)

    @pl.when(j == 0)
    def _():
        xf = x_ref[...]
        y = xf * lax.rsqrt(jnp.mean(xf * xf, axis=-1, keepdims=True) + NORM_EPS)
        h = (y * g_ref[...]).astype(BF16)
        h_ref[...] = h
        us_ref[...] = _dot(h, ws_ref[...])

    ub_ref[...] = _dot(h_ref[...], wb_ref[...]).astype(BF16)


def _in_proj(x2, g, w_big, w_small, tm, tn):
    t, d = x2.shape
    return pl.pallas_call(
        _in_proj_kernel,
        grid=(t // tm, N_BIG // tn),
        in_specs=[
            pl.BlockSpec((tm, d), lambda i, j: (i, 0)),
            pl.BlockSpec((1, d), lambda i, j: (0, 0)),
            pl.BlockSpec((d, tn), lambda i, j: (0, j)),
            pl.BlockSpec((d, N_SMALL), lambda i, j: (0, 0)),
        ],
        out_specs=[
            pl.BlockSpec((tm, tn), lambda i, j: (i, j)),
            pl.BlockSpec((tm, N_SMALL), lambda i, j: (i, 0)),
        ],
        out_shape=[
            jax.ShapeDtypeStruct((t, N_BIG), BF16),
            jax.ShapeDtypeStruct((t, N_SMALL), F32),
        ],
        scratch_shapes=[pltpu.VMEM((tm, d), BF16)],
        compiler_params=pltpu.CompilerParams(
            dimension_semantics=("arbitrary", "arbitrary"), vmem_limit_bytes=VMEM_LIMIT),
        name="in_proj",
    )(x2, g, w_big, w_small)


def _ssd_kernel(z_ref, xs_ref, bc_ref, sm_ref, cwx_ref, cwb_ref, cbx_ref, cbb_ref,
                dtb_ref, alog_ref, dsk_ref, ng_ref, e3_ref, tri_ref, it_ref, cm_ref, bm_ref,
                y_ref, xpx_ref, xpb_ref, st_ref, *, nch):
    lb = nch * CHUNK
    d_in = xs_ref.shape[-1]
    heads = d_in // SSM_HEAD_DIM
    gw = d_in // SSM_GROUPS
    gn = SSM_STATE

    @pl.when(pl.program_id(1) == 0)
    def _():
        xpx_ref[0:CONV_PAD, :] = jnp.zeros((CONV_PAD, d_in), F32)
        xpb_ref[0:CONV_PAD, :] = jnp.zeros((CONV_PAD, xpb_ref.shape[-1]), F32)
        st_ref[...] = jnp.zeros(st_ref.shape, F32)

    def conv(src_ref, pad_ref, w_ref, b_ref):
        pad_ref[CONV_PAD:CONV_PAD + lb, :] = src_ref[...].astype(F32)
        acc = b_ref[...] + w_ref[0:1, :] * pad_ref[CONV_PAD - 3:CONV_PAD - 3 + lb, :]
        for k in range(1, SSM_CONV):
            off = CONV_PAD - (SSM_CONV - 1) + k
            acc = acc + w_ref[k:k + 1, :] * pad_ref[off:off + lb, :]
        pad_ref[0:CONV_PAD, :] = pad_ref[lb:lb + CONV_PAD, :]
        return _silu(acc)

    xs = conv(xs_ref, xpx_ref, cwx_ref, cbx_ref)
    bc = conv(bc_ref, xpb_ref, cwb_ref, cbb_ref)

    dt_in = sm_ref[:, SM_DT:SM_DT + heads] + dtb_ref[...]
    dt = jnp.maximum(dt_in, 0.0) + jnp.log1p(jnp.exp(-jnp.abs(dt_in)))
    da = dt * (-jnp.exp(alog_ref[...]))
    tri = tri_ref[...]
    hi, mid, lo = _split3(da)
    a_cs = _dot(tri, hi) + _dot(tri, mid) + _dot(tri, lo)
    a_last = jnp.concatenate(
        [jnp.broadcast_to(a_cs[c * CHUNK + CHUNK - 1:c * CHUNK + CHUNK, :], (CHUNK, heads))
         for c in range(nch)], axis=0)
    stacked = jnp.concatenate([a_cs, jnp.exp(a_cs), jnp.exp(a_last - a_cs), dt], axis=0)
    s_hi, s_mid, s_lo = _split3(stacked)
    ex = _dot(jnp.concatenate([s_hi, s_mid, s_lo], axis=1), e3_ref[...])

    it_mask = it_ref[...] > 0.0
    cm_mask = cm_ref[...] > 0.0
    bm = bm_ref[...]
    it_bf = it_ref[...].astype(BF16)
    zf = z_ref[...].astype(F32)
    zg = _silu(zf)

    for c in range(nch):
        r0 = c * CHUNK
        a_l = ex[r0:r0 + CHUNK]
        exp_l = ex[lb + r0:lb + r0 + CHUNK]
        dst = ex[2 * lb + r0:2 * lb + r0 + CHUNK]
        dtx = ex[3 * lb + r0:3 * lb + r0 + CHUNK]
        xs_c = xs[r0:r0 + CHUNK]
        xdt = xs_c * dtx
        a_s = jnp.sum(jnp.where(it_mask, a_l, 0.0), axis=0, keepdims=True)
        decay = jnp.exp(jnp.where(cm_mask, a_l - a_s, -jnp.inf))
        xst = (xdt * dst).astype(BF16)
        xdt_bf = xdt.astype(BF16)
        y_parts = []
        for g in range(SSM_GROUPS):
            b_g = bc[r0:r0 + CHUNK, g * gn:(g + 1) * gn]
            c_g = bc[r0:r0 + CHUNK, (SSM_GROUPS + g) * gn:(SSM_GROUPS + g + 1) * gn]
            b_bf = b_g.astype(BF16)
            c_bf = c_g.astype(BF16)
            cb = _nt_dot(c_bf, b_bf)
            cbx = _dot(cb.astype(BF16), it_bf[:, 0:gw])
            w = (cbx * decay[:, g * gw:(g + 1) * gw]).astype(BF16)
            yd = []
            for jt in range(gw // 256):
                lo_c = g * gw + jt * 256
                xt = xdt_bf[:, lo_c:lo_c + 256]
                bd = jnp.concatenate([xt] * 4, axis=0) * bm
                yd.append(_dot(w[:, jt * 256:(jt + 1) * 256], bd))
            y_diag = jnp.concatenate(yd, axis=1)
            st_g = st_ref[g]
            y_off = _dot(c_bf, st_g.astype(BF16)) * exp_l[:, g * gw:(g + 1) * gw]
            new_st = _dot(jnp.transpose(b_g).astype(BF16), xst[:, g * gw:(g + 1) * gw])
            st_ref[g] = st_g * exp_l[CHUNK - 1:CHUNK, g * gw:(g + 1) * gw] + new_st
            y_g = y_diag + y_off + dsk_ref[:, g * gw:(g + 1) * gw] * xs_c[:, g * gw:(g + 1) * gw]
            y_g = y_g * zg[r0:r0 + CHUNK, g * gw:(g + 1) * gw]
            y_g = y_g * lax.rsqrt(jnp.mean(y_g * y_g, axis=-1, keepdims=True) + NORM_EPS)
            y_parts.append(y_g * ng_ref[:, g * gw:(g + 1) * gw])
        y_ref[r0:r0 + CHUNK, :] = jnp.concatenate(y_parts, axis=1).astype(BF16)


def _ssd(u_big, u_small, conv_w, conv_b, dt_bias, a_log, d_skip, ssm_norm_g, b, s, nch):
    t = b * s
    d_in = COL_BC - COL_XS
    n_bc = COL_Q - COL_BC
    heads = d_in // SSM_HEAD_DIM
    lb = nch * CHUNK
    nblk = s // lb
    assert d_in // SSM_GROUPS == 512 and SSM_HEAD_DIM == CHUNK

    e = np.kron(np.eye(heads, dtype=np.float32), np.ones((1, SSM_HEAD_DIM), np.float32))
    e3 = jnp.asarray(np.concatenate([e, e, e], axis=0), BF16)
    tri = jnp.asarray(np.kron(np.eye(nch, dtype=np.float32), np.tril(np.ones((CHUNK, CHUNK), np.float32))), BF16)
    it = jnp.asarray(np.tile(np.eye(CHUNK, dtype=np.float32), (1, heads)))
    cm = jnp.asarray(np.tile(np.tril(np.ones((CHUNK, CHUNK), np.float32)), (1, heads)))
    bm = jnp.asarray(np.kron(np.eye(4, dtype=np.float32), np.ones((64, 64), np.float32)), BF16)

    cwx, cwb = conv_w[:, :d_in], conv_w[:, d_in:]
    cbx, cbb = conv_b[None, :d_in], conv_b[None, d_in:]
    dsk = jnp.repeat(d_skip, SSM_HEAD_DIM)[None, :]
    const = lambda shape: pl.BlockSpec(shape, lambda bi, ci: (0,) * len(shape))
    row = lambda width, col: pl.BlockSpec((lb, width), lambda bi, ci: (bi * nblk + ci, col))

    return pl.pallas_call(
        functools.partial(_ssd_kernel, nch=nch),
        grid=(b, nblk),
        in_specs=[
            row(d_in, COL_Z // d_in), row(d_in, COL_XS // d_in), row(n_bc, COL_BC // n_bc),
            row(LANES, 1),
            const((SSM_CONV, d_in)), const((SSM_CONV, n_bc)), const((1, d_in)), const((1, n_bc)),
            const((1, heads)), const((1, heads)), const((1, d_in)), const((1, d_in)),
            const((3 * heads, d_in)), const((lb, lb)), const((CHUNK, d_in)), const((CHUNK, d_in)),
            const((256, 256)),
        ],
        out_specs=pl.BlockSpec((lb, d_in), lambda bi, ci: (bi * nblk + ci, 0)),
        out_shape=jax.ShapeDtypeStruct((t, d_in), BF16),
        scratch_shapes=[
            pltpu.VMEM((lb + CONV_PAD, d_in), F32),
            pltpu.VMEM((lb + CONV_PAD, n_bc), F32),
            pltpu.VMEM((SSM_GROUPS, SSM_STATE, d_in // SSM_GROUPS), F32),
        ],
        compiler_params=pltpu.CompilerParams(
            dimension_semantics=("arbitrary", "arbitrary"), vmem_limit_bytes=VMEM_LIMIT),
        name="ssd",
    )(u_big, u_big, u_big, u_small, cwx, cwb, cbx, cbb, dt_bias[None, :], a_log[None, :], dsk,
      ssm_norm_g[None, :], e3, tri, it, cm, bm)


def _kv_proj_kernel(c_ref, g_ref, wk_ref, wvt_ref, k_ref, vt_ref):
    cf = c_ref[...].astype(F32)
    y = cf * lax.rsqrt(jnp.mean(cf * cf, axis=-1, keepdims=True) + NORM_EPS)
    cn = (y * g_ref[...]).astype(BF16)
    k_ref[0] = (_dot(cn, wk_ref[...]) * LOG2E).astype(BF16)
    vt_ref[0] = _nt_dot(wvt_ref[...], cn).astype(BF16)


def _kv_proj(u_big, kv_norm_g, w_k, w_vt, b, s):
    width = ATT_HEADS * ATT_HEAD_DIM
    return pl.pallas_call(
        _kv_proj_kernel,
        grid=(b,),
        in_specs=[
            pl.BlockSpec((s, KV_LATENT), lambda bi: (bi, COL_CKV // KV_LATENT)),
            pl.BlockSpec((1, KV_LATENT), lambda bi: (0, 0)),
            pl.BlockSpec((KV_LATENT, width), lambda bi: (0, 0)),
            pl.BlockSpec((width, KV_LATENT), lambda bi: (0, 0)),
        ],
        out_specs=[
            pl.BlockSpec((1, s, width), lambda bi: (bi, 0, 0)),
            pl.BlockSpec((1, width, s), lambda bi: (bi, 0, 0)),
        ],
        out_shape=[
            jax.ShapeDtypeStruct((b, s, width), BF16),
            jax.ShapeDtypeStruct((b, width, s), BF16),
        ],
        compiler_params=pltpu.CompilerParams(
            dimension_semantics=("arbitrary",), vmem_limit_bytes=VMEM_LIMIT),
        name="kv_proj",
    )(u_big, kv_norm_g, w_k, w_vt)


def _col_reduce(x, op):
    n = x.shape[0]
    slab = op(x.reshape(n // ROW_SLAB, ROW_SLAB, x.shape[1]), axis=0)
    return op(slab, axis=0, keepdims=True)


def _dsa_kernel(q_ref, qi_ref, smq_ref, smk_ref, k_ref, vt_ref, slope_ref, o_ref,
                kidx_ref, sc_ref, thr_ref, jcut_ref, wt_ref, q2_ref, qi2_ref, m_ref, l_ref, acc_ref, ot_ref,
                att_ref, p_ref, alpha_ref, *, n_buckets, topk):
    i = pl.program_id(1)

    @pl.when(i == 0)
    def _():
        kidx_ref[...] = smk_ref[...].astype(BF16)

    lane = lax.broadcasted_iota(jnp.int32, (Q_TILE, LANES), 1)
    low_half = lane < IDX_HEAD_DIM

    def head_pair_rows(x):
        zero = jnp.zeros_like(x)
        return jnp.concatenate([jnp.where(low_half, x, zero), jnp.where(low_half, zero, x)], axis=0)

    wt_ref[...] = jnp.transpose(smq_ref[...])[SM_W:SM_W + IDX_HEADS, :] * (IDX_HEADS ** -0.5 * IDX_HEAD_DIM ** -0.5)
    for pair in range(IDX_HEADS // 2):
        qi2_ref[pair] = head_pair_rows(qi_ref[:, pair * LANES:(pair + 1) * LANES])
    for pair in range(ATT_HEADS // 2):
        q2_ref[pair] = head_pair_rows(
            q_ref[:, pair * LANES:(pair + 1) * LANES] * jnp.asarray(ATT_HEAD_DIM ** -0.5, BF16))
    kf = float(topk)

    n_chunks = ((i + 1) * Q_TILE + KEY_CHUNK - 1) // KEY_CHUNK
    row_iota = lax.broadcasted_iota(jnp.int32, (KEY_CHUNK, Q_TILE), 0)
    t_pos = i * Q_TILE + lax.broadcasted_iota(jnp.int32, (KEY_CHUNK, Q_TILE), 1)
    vis_end = (t_pos // CHUNK + 1) * CHUNK

    def chunk_rows(c):
        return pl.ds(pl.multiple_of(c * KEY_CHUNK, KEY_CHUNK), KEY_CHUNK)

    def index_chunk(c, carry):
        rows = chunk_rows(c)
        score = jnp.zeros((KEY_CHUNK, Q_TILE), F32)
        for pair in range(IDX_HEADS // 2):
            logit = _nt_dot(kidx_ref[rows, :], qi2_ref[pair])
            score = (score + jnp.maximum(logit[:, 0:Q_TILE], 0.0) * wt_ref[2 * pair:2 * pair + 1, :]
                     + jnp.maximum(logit[:, Q_TILE:], 0.0) * wt_ref[2 * pair + 1:2 * pair + 2, :])
        sc_ref[rows, :] = jnp.where(c * KEY_CHUNK + row_iota < vis_end, score, -jnp.inf)
        return carry

    lax.fori_loop(0, n_chunks, index_chunk, 0)

    many = vis_end[0:1, :] > topk

    def select(n):
        s_pos = lax.broadcasted_iota(jnp.int32, (n, Q_TILE), 0)

        def count(pred):
            return _col_reduce(jnp.where(pred, 1.0, 0.0), jnp.sum)

        def bit_step(it, u):
            cand = u | (jnp.int32(1) << (31 - it))
            cand_f = pltpu.bitcast(jnp.where(cand < 0, cand ^ INT_MIN, ~cand), F32)
            c = count(sc_ref[0:n, :] >= cand_f)
            return jnp.where(c >= kf, cand, u)

        u_thr = lax.fori_loop(0, 32, bit_step, jnp.zeros((1, Q_TILE), jnp.int32))
        thr = pltpu.bitcast(jnp.where(u_thr < 0, u_thr ^ INT_MIN, ~u_thr), F32)
        thr = jnp.where(many, thr, -jnp.inf)
        thr_ref[...] = thr
        sc = sc_ref[0:n, :]
        gt = sc > thr
        eq = sc == thr
        need = kf - count(gt)
        excess = jnp.where(many, count(eq) - need, 0.0)
        jcut_ref[...] = jnp.where(many, n, -1)

        @pl.when(jnp.max(excess) > 0.0)
        def _():
            nbits = max(1, int(np.ceil(np.log2(n))))

            def idx_step(it, j0):
                cand = j0 | (jnp.int32(1) << (nbits - 1 - it))
                c = count((sc_ref[0:n, :] == thr) & (s_pos < cand))
                return jnp.where(c < need, cand, j0)

            j0 = lax.fori_loop(0, nbits, idx_step, jnp.zeros((1, Q_TILE), jnp.int32))
            jcut_ref[...] = jnp.where(many, j0, -1)

    for nb in range(1, n_buckets + 1):
        pl.when(n_chunks == nb)(functools.partial(select, nb * KEY_CHUNK))

    m_ref[...] = jnp.full(m_ref.shape, SOFTMAX_M0, F32)
    l_ref[...] = jnp.zeros(l_ref.shape, F32)
    acc_ref[...] = jnp.zeros(acc_ref.shape, F32)
    hd = ATT_HEAD_DIM

    def attend_chunk(c, carry):
        rows = chunk_rows(c)
        s_pos = c * KEY_CHUNK + row_iota
        sc = sc_ref[rows, :]
        sel = (sc > thr_ref[...]) | ((sc == thr_ref[...]) & (s_pos <= jcut_ref[...]))
        dsel = jnp.where(sel, jnp.abs(t_pos - s_pos).astype(F32), jnp.inf)
        for pair in range(ATT_HEADS // 2):
            att_ref[pair] = _nt_dot(k_ref[0, rows, pair * LANES:(pair + 1) * LANES], q2_ref[pair])
        for h in range(ATT_HEADS):
            pair, half = divmod(h, 2)
            cols = slice(half * Q_TILE, (half + 1) * Q_TILE)
            a = att_ref[pair, :, cols] - slope_ref[h:h + 1, :] * dsel
            m_old = m_ref[h:h + 1, :]
            m_new = jnp.maximum(m_old, _col_reduce(a, jnp.max))
            alpha = jnp.exp2(m_old - m_new)
            p = jnp.exp2(a - m_new)
            l_ref[h:h + 1, :] = alpha * l_ref[h:h + 1, :] + _col_reduce(p, jnp.sum)
            m_ref[h:h + 1, :] = m_new
            alpha_ref[h:h + 1, :] = alpha
            p_ref[pair, :, cols] = p.astype(BF16)
        for pair in range(ATT_HEADS // 2):
            o2 = _dot(vt_ref[0, pair * LANES:(pair + 1) * LANES, rows], p_ref[pair])
            for half in range(2):
                h = 2 * pair + half
                blk = o2[half * hd:(half + 1) * hd, half * Q_TILE:(half + 1) * Q_TILE]
                acc_ref[h] = acc_ref[h] * alpha_ref[h:h + 1, :] + blk
        return carry

    lax.fori_loop(0, n_chunks, attend_chunk, 0)
    for h in range(ATT_HEADS):
        ot_ref[h * hd:(h + 1) * hd, :] = acc_ref[h] * (1.0 / l_ref[h:h + 1, :])
    o_ref[...] = jnp.transpose(ot_ref[...]).astype(BF16)


def _dsa(u_big, u_small, k_all, vt_all, b, s):
    t = b * s
    width = ATT_HEADS * ATT_HEAD_DIM
    nq = s // Q_TILE
    topk = min(TOPK_MAX, s // 4)
    assert s % KEY_CHUNK == 0
    slopes = 2.0 ** (-8.0 * jnp.arange(1, ATT_HEADS + 1, dtype=F32) / ATT_HEADS) * LOG2E
    slopes = jnp.broadcast_to(slopes[:, None], (ATT_HEADS, Q_TILE))
    qi_w = IDX_HEADS * IDX_HEAD_DIM
    return pl.pallas_call(
        functools.partial(_dsa_kernel, n_buckets=s // KEY_CHUNK, topk=topk),
        grid=(b, nq),
        in_specs=[
            pl.BlockSpec((Q_TILE, width), lambda bi, qi: (bi * nq + qi, COL_Q // width)),
            pl.BlockSpec((Q_TILE, qi_w), lambda bi, qi: (bi * nq + qi, COL_QI // qi_w)),
            pl.BlockSpec((Q_TILE, LANES), lambda bi, qi: (bi * nq + qi, 1)),
            pl.BlockSpec((s, LANES), lambda bi, qi: (bi, 0)),
            pl.BlockSpec((1, s, width), lambda bi, qi: (bi, 0, 0)),
            pl.BlockSpec((1, width, s), lambda bi, qi: (bi, 0, 0)),
            pl.BlockSpec((ATT_HEADS, Q_TILE), lambda bi, qi: (0, 0)),
        ],
        out_specs=pl.BlockSpec((Q_TILE, width), lambda bi, qi: (bi * nq + qi, 0)),
        out_shape=jax.ShapeDtypeStruct((t, width), BF16),
        scratch_shapes=[
            pltpu.VMEM((s, LANES), BF16),
            pltpu.VMEM((s, Q_TILE), F32),
            pltpu.VMEM((1, Q_TILE), F32),
            pltpu.VMEM((1, Q_TILE), jnp.int32),
            pltpu.VMEM((IDX_HEADS, Q_TILE), F32),
            pltpu.VMEM((ATT_HEADS // 2, 2 * Q_TILE, LANES), BF16),
            pltpu.VMEM((IDX_HEADS // 2, 2 * Q_TILE, LANES), BF16),
            pltpu.VMEM((ATT_HEADS, Q_TILE), F32),
            pltpu.VMEM((ATT_HEADS, Q_TILE), F32),
            pltpu.VMEM((ATT_HEADS, ATT_HEAD_DIM, Q_TILE), F32),
            pltpu.VMEM((width, Q_TILE), F32),
            pltpu.VMEM((ATT_HEADS // 2, KEY_CHUNK, 2 * Q_TILE), F32),
            pltpu.VMEM((ATT_HEADS // 2, KEY_CHUNK, 2 * Q_TILE), BF16),
            pltpu.VMEM((ATT_HEADS, Q_TILE), F32),
        ],
        compiler_params=pltpu.CompilerParams(
            dimension_semantics=("arbitrary", "arbitrary"), vmem_limit_bytes=VMEM_LIMIT),
        name="dsa",
    )(u_big, u_big, u_small, u_small, k_all, vt_all, slopes)


def _merge_kernel(x_ref, ys_ref, ya_ref, g0_ref, g1_ref, gb_ref, wbs_ref, wba_ref, wo_ref,
                  nf_ref, wr_ref, br_ref, x1_ref, h2_ref, gate_ref):
    def sig(v):
        return 1.0 / (1.0 + jnp.exp(-v))

    g0 = sig(g0_ref[...].astype(F32) + gb_ref[0:1, :])
    g1 = sig(g1_ref[...].astype(F32) + gb_ref[1:2, :])
    mixed = g0 * _dot(ys_ref[...], wbs_ref[...]) + g1 * _dot(ya_ref[...], wba_ref[...])
    x1 = x_ref[...] + _dot(mixed.astype(BF16), wo_ref[...])
    x1_ref[...] = x1
    h2 = x1 * lax.rsqrt(jnp.mean(x1 * x1, axis=-1, keepdims=True) + NORM_EPS) * nf_ref[...]
    h2b = h2.astype(BF16)
    h2_ref[...] = h2b

    n_e = MOE_GROUPS * MOE_EXPERTS
    logits = _dot(h2b, wr_ref[...]) + br_ref[...]
    lane = lax.broadcasted_iota(jnp.int32, logits.shape, 1)
    neg = -jnp.inf
    is_g = (lane >= n_e) & (lane < n_e + MOE_GROUPS)
    gl = jnp.where(is_g, logits, neg)
    gmax = jnp.max(gl, axis=-1, keepdims=True)
    g_val = 1.0 / jnp.sum(jnp.exp(gl - gmax), axis=-1, keepdims=True)
    g_sel = jnp.min(jnp.where(is_g & (gl == gmax), lane, 4 * LANES), axis=-1, keepdims=True) - n_e
    in_grp = (lane < n_e) & ((lane // MOE_EXPERTS) == g_sel)
    el = jnp.where(in_grp, logits, neg)
    m1 = jnp.max(el, axis=-1, keepdims=True)
    i1 = jnp.min(jnp.where(in_grp & (el == m1), lane, 4 * LANES), axis=-1, keepdims=True)
    el2 = jnp.where(lane == i1, neg, el)
    m2 = jnp.max(el2, axis=-1, keepdims=True)
    i2 = jnp.min(jnp.where(in_grp & (el2 == m2), lane, 4 * LANES), axis=-1, keepdims=True)
    p2 = jnp.exp(m2 - m1)
    inv = 1.0 / (1.0 + p2)
    gate = jnp.where(lane == i1, inv, 0.0) + jnp.where(lane == i2, p2 * inv, 0.0)
    gate_ref[...] = gate * g_val


def _merge(x2, y_ssm, y_att, u_big, gate_b, w_bs, w_ba, w_o, norm_ffn_g, w_r, b_r, tm):
    t, d = x2.shape
    d_in = y_ssm.shape[1]
    const = lambda shape: pl.BlockSpec(shape, lambda i: (0,) * len(shape))
    return pl.pallas_call(
        _merge_kernel,
        grid=(t // tm,),
        in_specs=[
            pl.BlockSpec((tm, d), lambda i: (i, 0)),
            pl.BlockSpec((tm, d_in), lambda i: (i, 0)),
            pl.BlockSpec((tm, d), lambda i: (i, 0)),
            pl.BlockSpec((tm, d), lambda i: (i, COL_G // d)),
            pl.BlockSpec((tm, d), lambda i: (i, COL_G // d + 1)),
            const((2, d)), const((d_in, d)), const((d, d)), const((d, d)), const((1, d)),
            const((d, LANES)), const((1, LANES)),
        ],
        out_specs=[
            pl.BlockSpec((tm, d), lambda i: (i, 0)),
            pl.BlockSpec((tm, d), lambda i: (i, 0)),
            pl.BlockSpec((tm, LANES), lambda i: (i, 0)),
        ],
        out_shape=[
            jax.ShapeDtypeStruct((t, d), F32),
            jax.ShapeDtypeStruct((t, d), BF16),
            jax.ShapeDtypeStruct((t, LANES), F32),
        ],
        compiler_params=pltpu.CompilerParams(
            dimension_semantics=("arbitrary",), vmem_limit_bytes=VMEM_LIMIT),
        name="merge",
    )(x2, y_ssm, y_att, u_big, u_big, gate_b, w_bs, w_ba, w_o, norm_ffn_g, w_r, b_r)


def _moe_kernel(h_ref, gate_ref, x1_ref, w1_ref, w3_ref, w2_ref, nf_ref, o_ref, acc_ref):
    e = pl.program_id(1)

    @pl.when(e == 0)
    def _():
        acc_ref[...] = x1_ref[...]

    h = h_ref[...]
    a = _dot(h, w1_ref[0])
    bgate = _dot(h, w3_ref[0])
    lane = lax.broadcasted_iota(jnp.int32, gate_ref.shape, 1)
    g_col = jnp.sum(jnp.where(lane == e, gate_ref[...], 0.0), axis=-1, keepdims=True)
    hid = _silu(a) * bgate * g_col
    acc_ref[...] += _dot(hid.astype(BF16), w2_ref[0])

    @pl.when(e == pl.num_programs(1) - 1)
    def _():
        x2 = acc_ref[...]
        o_ref[...] = x2 * lax.rsqrt(jnp.mean(x2 * x2, axis=-1, keepdims=True) + NORM_EPS) * nf_ref[...]


def _moe(h2, gate, x1, w1, w3, w2, norm_final_g, tm):
    t, d = x1.shape
    n_e, _, ff = w1.shape
    return pl.pallas_call(
        _moe_kernel,
        grid=(t // tm, n_e),
        in_specs=[
            pl.BlockSpec((tm, d), lambda i, e: (i, 0)),
            pl.BlockSpec((tm, LANES), lambda i, e: (i, 0)),
            pl.BlockSpec((tm, d), lambda i, e: (i, 0)),
            pl.BlockSpec((1, d, ff), lambda i, e: (e, 0, 0)),
            pl.BlockSpec((1, d, ff), lambda i, e: (e, 0, 0)),
            pl.BlockSpec((1, ff, d), lambda i, e: (e, 0, 0)),
            pl.BlockSpec((1, d), lambda i, e: (0, 0)),
        ],
        out_specs=pl.BlockSpec((tm, d), lambda i, e: (i, 0)),
        out_shape=jax.ShapeDtypeStruct((t, d), F32),
        scratch_shapes=[pltpu.VMEM((tm, d), F32)],
        compiler_params=pltpu.CompilerParams(
            dimension_semantics=("arbitrary", "arbitrary"), vmem_limit_bytes=VMEM_LIMIT),
        name="moe",
    )(h2, gate, x1, w1, w3, w2, norm_final_g)


def _pick_tile(t, pref):
    tm = pref
    while t % tm:
        tm //= 2
    return tm


def kernel(x, norm_mix_g, w_in, conv_w, conv_b, dt_bias, a_log, d_skip, ssm_norm_g, kv_norm_g, w_uk, w_uv,
           gate_b, w_branch_ssm, w_branch_att, w_out, norm_ffn_g, w_group, b_group, w_router, b_router,
           w_e1, w_e3, w_e2, norm_final_g):
    b, s, d = x.shape
    depth = w_in.shape[0]
    t = b * s
    d_in = 2 * d
    heads = d_in // SSM_HEAD_DIM
    n_bc = 2 * SSM_GROUPS * SSM_STATE
    width = ATT_HEADS * ATT_HEAD_DIM
    qi_w = IDX_HEADS * IDX_HEAD_DIM
    assert (d, d_in, n_bc, width) == (1024, COL_XS - COL_Z, COL_Q - COL_BC, COL_QI - COL_Q)
    assert s % (2 * CHUNK) == 0 and s % Q_TILE == 0

    assert depth == 1, "the final RMSNorm is fused into the expert kernel of the only layer"
    layer = 0
    x2 = x.reshape(t, d)

    sizes = (d_in, d_in + n_bc, heads, width, KV_LATENT, qi_w, IDX_HEAD_DIM, IDX_HEADS, 2 * d)
    cuts = np.concatenate([[0], np.cumsum(sizes)])
    seg = lambda k: w_in[layer][:, int(cuts[k]):int(cuts[k + 1])]
    w_z, w_xbc, w_dt, w_q, w_ckv, w_qi, w_ki, w_wi, w_g = (seg(k) for k in range(9))
    zpad = lambda n: jnp.zeros((d, n), w_in.dtype)
    w_big = jnp.concatenate(
        [w_z, w_xbc, w_q, w_qi, w_ckv, zpad(COL_G - COL_CKV - KV_LATENT), w_g], axis=1).astype(BF16)
    w_small = jnp.concatenate(
        [w_ki, w_ki, w_dt, w_wi, zpad(N_SMALL - 2 * IDX_HEAD_DIM - heads - IDX_HEADS)], axis=1).astype(BF16)
    assert w_big.shape[1] == N_BIG and w_small.shape[1] == N_SMALL

    u_big, u_small = _in_proj(x2, norm_mix_g[layer][None, :], w_big, w_small, _pick_tile(t, 1024), 1536)

    y_ssm = _ssd(u_big, u_small, conv_w[layer], conv_b[layer], dt_bias[layer], a_log[layer],
                 d_skip[layer], ssm_norm_g[layer], b, s, 2)

    w_k = jnp.transpose(w_uk[layer], (1, 0, 2)).reshape(KV_LATENT, width).astype(BF16)
    w_vt = jnp.transpose(w_uv[layer], (0, 2, 1)).reshape(width, KV_LATENT).astype(BF16)
    k_all, vt_all = _kv_proj(u_big, kv_norm_g[layer][None, :], w_k, w_vt, b, s)
    y_att = _dsa(u_big, u_small, k_all, vt_all, b, s)

    n_e = MOE_GROUPS * MOE_EXPERTS
    w_r = jnp.concatenate(
        [jnp.transpose(w_router[layer], (1, 0, 2)).reshape(d, n_e), w_group[layer],
         jnp.zeros((d, LANES - n_e - MOE_GROUPS), F32)], axis=1).astype(BF16)
    b_r = jnp.concatenate(
        [b_router[layer].reshape(n_e), b_group[layer], jnp.zeros((LANES - n_e - MOE_GROUPS,), F32)])[None, :]
    x1, h2, gate = _merge(x2, y_ssm, y_att, u_big, gate_b[layer],
                          w_branch_ssm[layer].astype(BF16), w_branch_att[layer].astype(BF16),
                          w_out[layer].astype(BF16), norm_ffn_g[layer][None, :], w_r, b_r,
                          _pick_tile(t, 512))

    out = _moe(h2, gate, x1,
               w_e1[layer].reshape(n_e, d, MOE_FF).astype(BF16),
               w_e3[layer].reshape(n_e, d, MOE_FF).astype(BF16),
               w_e2[layer].reshape(n_e, MOE_FF, d).astype(BF16),
               norm_final_g[None, :], _pick_tile(t, 1024))
    return out.reshape(b, s, d)
```

```python
import functools

import numpy as np
import jax
import jax.numpy as jnp
from jax import lax
from jax.experimental import pallas as pl
from jax.experimental.pallas import tpu as pltpu

F32 = jnp.float32
BF16 = jnp.bfloat16

NORM_EPS = 1e-6
CHUNK = 64
SSM_HEAD_DIM = 64
SSM_GROUPS = 4
SSM_STATE = 128
SSM_CONV = 4
ATT_HEADS = 16
ATT_HEAD_DIM = 64
KV_LATENT = 256
IDX_HEADS = 8
IDX_HEAD_DIM = 64
TOPK_MAX = 256
MOE_GROUPS = 4
MOE_EXPERTS = 8
MOE_FF = 256

LANES = 128
Q_TILE = 128
CONV_PAD = 8
VMEM_LIMIT = 56 * 1024 * 1024
INT_MIN = -(2 ** 31)
LOG2E = 1.4426950408889634
ROW_SLAB = 64
KEY_CHUNK = 256
VT_ONES = 16
SOFTMAX_M0 = -1e30

COL_Z, COL_XS, COL_BC, COL_Q, COL_QI, COL_CKV, COL_G = 0, 2048, 4096, 5120, 6144, 6656, 7168
N_BIG = 9216
N_SMALL = 256
SM_DT, SM_W = 0, 32


def _nt_dot(a, b):
    return lax.dot_general(a, b, (((1,), (1,)), ((), ())), preferred_element_type=F32)


def _dot(a, b):
    return jnp.dot(a, b, preferred_element_type=F32)


def _split3(a):
    hi = a.astype(BF16)
    r = a - hi.astype(F32)
    mid = r.astype(BF16)
    lo = (r - mid.astype(F32)).astype(BF16)
    return hi, mid, lo


def _silu(v):
    hv = 0.5 * v
    return hv + hv * jnp.tanh(hv)


def _in_proj_kernel(x_ref, g_ref, wb_ref, ws_ref, ub_ref, us_ref, h_ref):
    j = pl.program_id(1)

    @pl.when(j == 0)
    def _():
        xf = x_ref[...]
        y = xf * lax.rsqrt(jnp.mean(xf * xf, axis=-1, keepdims=True) + NORM_EPS)
        h = (y * g_ref[...]).astype(BF16)
        h_ref[...] = h
        us_ref[...] = _dot(h, ws_ref[...])

    ub_ref[...] = _dot(h_ref[...], wb_ref[...]).astype(BF16)


def _in_proj(x2, g, w_big, w_small, tm, tn):
    t, d = x2.shape
    return pl.pallas_call(
        _in_proj_kernel,
        grid=(t // tm, N_BIG // tn),
        in_specs=[
            pl.BlockSpec((tm, d), lambda i, j: (i, 0)),
            pl.BlockSpec((1, d), lambda i, j: (0, 0)),
            pl.BlockSpec((d, tn), lambda i, j: (0, j)),
            pl.BlockSpec((d, N_SMALL), lambda i, j: (0, 0)),
        ],
        out_specs=[
            pl.BlockSpec((tm, tn), lambda i, j: (i, j)),
            pl.BlockSpec((tm, N_SMALL), lambda i, j: (i, 0)),
        ],
        out_shape=[
            jax.ShapeDtypeStruct((t, N_BIG), BF16),
            jax.ShapeDtypeStruct((t, N_SMALL), F32),
        ],
        scratch_shapes=[pltpu.VMEM((tm, d), BF16)],
        compiler_params=pltpu.CompilerParams(
            dimension_semantics=("arbitrary", "arbitrary"), vmem_limit_bytes=VMEM_LIMIT),
        name="in_proj",
    )(x2, g, w_big, w_small)


def _ssd_kernel(z_ref, xs_ref, bc_ref, sm_ref, cwx_ref, cwb_ref, cbx_ref, cbb_ref,
                dtb_ref, alog_ref, dsk_ref, ng_ref, e3_ref, tri_ref, it_ref, cm_ref, bm_ref,
                y_ref, xpx_ref, xpb_ref, st_ref, *, nch):
    lb = nch * CHUNK
    d_in = xs_ref.shape[-1]
    heads = d_in // SSM_HEAD_DIM
    gw = d_in // SSM_GROUPS
    gn = SSM_STATE

    @pl.when(pl.program_id(1) == 0)
    def _():
        xpx_ref[0:CONV_PAD, :] = jnp.zeros((CONV_PAD, d_in), F32)
        xpb_ref[0:CONV_PAD, :] = jnp.zeros((CONV_PAD, xpb_ref.shape[-1]), F32)
        st_ref[...] = jnp.zeros(st_ref.shape, F32)

    def conv(src_ref, pad_ref, w_ref, b_ref):
        pad_ref[CONV_PAD:CONV_PAD + lb, :] = src_ref[...].astype(F32)
        acc = b_ref[...] + w_ref[0:1, :] * pad_ref[CONV_PAD - 3:CONV_PAD - 3 + lb, :]
        for k in range(1, SSM_CONV):
            off = CONV_PAD - (SSM_CONV - 1) + k
            acc = acc + w_ref[k:k + 1, :] * pad_ref[off:off + lb, :]
        pad_ref[0:CONV_PAD, :] = pad_ref[lb:lb + CONV_PAD, :]
        return _silu(acc)

    xs = conv(xs_ref, xpx_ref, cwx_ref, cbx_ref)
    bc = conv(bc_ref, xpb_ref, cwb_ref, cbb_ref)

    dt_in = sm_ref[:, SM_DT:SM_DT + heads] + dtb_ref[...]
    dt = jnp.maximum(dt_in, 0.0) + jnp.log1p(jnp.exp(-jnp.abs(dt_in)))
    da = dt * (-jnp.exp(alog_ref[...]))
    tri = tri_ref[...]
    hi, mid, lo = _split3(da)
    a_cs = _dot(tri, hi) + _dot(tri, mid) + _dot(tri, lo)
    a_last = jnp.concatenate(
        [jnp.broadcast_to(a_cs[c * CHUNK + CHUNK - 1:c * CHUNK + CHUNK, :], (CHUNK, heads))
         for c in range(nch)], axis=0)
    stacked = jnp.concatenate([a_cs, jnp.exp(a_cs), jnp.exp(a_last - a_cs), dt], axis=0)
    s_hi, s_mid, s_lo = _split3(stacked)
    ex = _dot(jnp.concatenate([s_hi, s_mid, s_lo], axis=1), e3_ref[...])

    it_mask = it_ref[...] > 0.0
    cm_mask = cm_ref[...] > 0.0
    bm = bm_ref[...]
    it_bf = it_ref[...].astype(BF16)
    zf = z_ref[...].astype(F32)
    zg = _silu(zf)

    for c in range(nch):
        r0 = c * CHUNK
        a_l = ex[r0:r0 + CHUNK]
        exp_l = ex[lb + r0:lb + r0 + CHUNK]
        dst = ex[2 * lb + r0:2 * lb + r0 + CHUNK]
        dtx = ex[3 * lb + r0:3 * lb + r0 + CHUNK]
        xs_c = xs[r0:r0 + CHUNK]
        xdt = xs_c * dtx
        a_s = jnp.sum(jnp.where(it_mask, a_l, 0.0), axis=0, keepdims=True)
        decay = jnp.exp(jnp.where(cm_mask, a_l - a_s, -jnp.inf))
        xst = (xdt * dst).astype(BF16)
        xdt_bf = xdt.astype(BF16)
        y_parts = []
        for g in range(SSM_GROUPS):
            b_g = bc[r0:r0 + CHUNK, g * gn:(g + 1) * gn]
            c_g = bc[r0:r0 + CHUNK, (SSM_GROUPS + g) * gn:(SSM_GROUPS + g + 1) * gn]
            b_bf = b_g.astype(BF16)
            c_bf = c_g.astype(BF16)
            cb = _nt_dot(c_bf, b_bf)
            cbx = _dot(cb.astype(BF16), it_bf[:, 0:gw])
            w = (cbx * decay[:, g * gw:(g + 1) * gw]).astype(BF16)
            yd = []
            for jt in range(gw // 256):
                lo_c = g * gw + jt * 256
                xt = xdt_bf[:, lo_c:lo_c + 256]
                bd = jnp.concatenate([xt] * 4, axis=0) * bm
                yd.append(_dot(w[:, jt * 256:(jt + 1) * 256], bd))
            y_diag = jnp.concatenate(yd, axis=1)
            st_g = st_ref[g]
            y_off = _dot(c_bf, st_g.astype(BF16)) * exp_l[:, g * gw:(g + 1) * gw]
            new_st = _dot(jnp.transpose(b_g).astype(BF16), xst[:, g * gw:(g + 1) * gw])
            st_ref[g] = st_g * exp_l[CHUNK - 1:CHUNK, g * gw:(g + 1) * gw] + new_st
            y_g = y_diag + y_off + dsk_ref[:, g * gw:(g + 1) * gw] * xs_c[:, g * gw:(g + 1) * gw]
            y_g = y_g * zg[r0:r0 + CHUNK, g * gw:(g + 1) * gw]
            y_g = y_g * lax.rsqrt(jnp.mean(y_g * y_g, axis=-1, keepdims=True) + NORM_EPS)
            y_parts.append(y_g * ng_ref[:, g * gw:(g + 1) * gw])
        y_ref[r0:r0 + CHUNK, :] = jnp.concatenate(y_parts, axis=1).astype(BF16)


def _ssd(u_big, u_small, conv_w, conv_b, dt_bias, a_log, d_skip, ssm_norm_g, b, s, nch):
    t = b * s
    d_in = COL_BC - COL_XS
    n_bc = COL_Q - COL_BC
    heads = d_in // SSM_HEAD_DIM
    lb = nch * CHUNK
    nblk = s // lb
    assert d_in // SSM_GROUPS == 512 and SSM_HEAD_DIM == CHUNK

    e = np.kron(np.eye(heads, dtype=np.float32), np.ones((1, SSM_HEAD_DIM), np.float32))
    e3 = jnp.asarray(np.concatenate([e, e, e], axis=0), BF16)
    tri = jnp.asarray(np.kron(np.eye(nch, dtype=np.float32), np.tril(np.ones((CHUNK, CHUNK), np.float32))), BF16)
    it = jnp.asarray(np.tile(np.eye(CHUNK, dtype=np.float32), (1, heads)))
    cm = jnp.asarray(np.tile(np.tril(np.ones((CHUNK, CHUNK), np.float32)), (1, heads)))
    bm = jnp.asarray(np.kron(np.eye(4, dtype=np.float32), np.ones((64, 64), np.float32)), BF16)

    cwx, cwb = conv_w[:, :d_in], conv_w[:, d_in:]
    cbx, cbb = conv_b[None, :d_in], conv_b[None, d_in:]
    dsk = jnp.repeat(d_skip, SSM_HEAD_DIM)[None, :]
    const = lambda shape: pl.BlockSpec(shape, lambda bi, ci: (0,) * len(shape))
    row = lambda width, col: pl.BlockSpec((lb, width), lambda bi, ci: (bi * nblk + ci, col))

    return pl.pallas_call(
        functools.partial(_ssd_kernel, nch=nch),
        grid=(b, nblk),
        in_specs=[
            row(d_in, COL_Z // d_in), row(d_in, COL_XS // d_in), row(n_bc, COL_BC // n_bc),
            row(LANES, 1),
            const((SSM_CONV, d_in)), const((SSM_CONV, n_bc)), const((1, d_in)), const((1, n_bc)),
            const((1, heads)), const((1, heads)), const((1, d_in)), const((1, d_in)),
            const((3 * heads, d_in)), const((lb, lb)), const((CHUNK, d_in)), const((CHUNK, d_in)),
            const((256, 256)),
        ],
        out_specs=pl.BlockSpec((lb, d_in), lambda bi, ci: (bi * nblk + ci, 0)),
        out_shape=jax.ShapeDtypeStruct((t, d_in), BF16),
        scratch_shapes=[
            pltpu.VMEM((lb + CONV_PAD, d_in), F32),
            pltpu.VMEM((lb + CONV_PAD, n_bc), F32),
            pltpu.VMEM((SSM_GROUPS, SSM_STATE, d_in // SSM_GROUPS), F32),
        ],
        compiler_params=pltpu.CompilerParams(
            dimension_semantics=("arbitrary", "arbitrary"), vmem_limit_bytes=VMEM_LIMIT),
        name="ssd",
    )(u_big, u_big, u_big, u_small, cwx, cwb, cbx, cbb, dt_bias[None, :], a_log[None, :], dsk,
      ssm_norm_g[None, :], e3, tri, it, cm, bm)


def _kv_proj_kernel(c_ref, g_ref, wk_ref, wvt_ref, k_ref, vt_ref):
    cf = c_ref[...].astype(F32)
    y = cf * lax.rsqrt(jnp.mean(cf * cf, axis=-1, keepdims=True) + NORM_EPS)
    cn = (y * g_ref[...]).astype(BF16)
    k_ref[0] = (_dot(cn, wk_ref[...]) * LOG2E).astype(BF16)
    vt = _nt_dot(wvt_ref[...], cn).astype(BF16)
    for pair in range(ATT_HEADS // 2):
        vt_ref[0, pair, 0:LANES, :] = vt[pair * LANES:(pair + 1) * LANES, :]
        vt_ref[0, pair, LANES:, :] = jnp.ones((VT_ONES, vt.shape[1]), BF16)


def _kv_proj(u_big, kv_norm_g, w_k, w_vt, b, s):
    width = ATT_HEADS * ATT_HEAD_DIM
    return pl.pallas_call(
        _kv_proj_kernel,
        grid=(b,),
        in_specs=[
            pl.BlockSpec((s, KV_LATENT), lambda bi: (bi, COL_CKV // KV_LATENT)),
            pl.BlockSpec((1, KV_LATENT), lambda bi: (0, 0)),
            pl.BlockSpec((KV_LATENT, width), lambda bi: (0, 0)),
            pl.BlockSpec((width, KV_LATENT), lambda bi: (0, 0)),
        ],
        out_specs=[
            pl.BlockSpec((1, s, width), lambda bi: (bi, 0, 0)),
            pl.BlockSpec((1, ATT_HEADS // 2, LANES + VT_ONES, s), lambda bi: (bi, 0, 0, 0)),
        ],
        out_shape=[
            jax.ShapeDtypeStruct((b, s, width), BF16),
            jax.ShapeDtypeStruct((b, ATT_HEADS // 2, LANES + VT_ONES, s), BF16),
        ],
        compiler_params=pltpu.CompilerParams(
            dimension_semantics=("arbitrary",), vmem_limit_bytes=VMEM_LIMIT),
        name="kv_proj",
    )(u_big, kv_norm_g, w_k, w_vt)


def _col_reduce(x, op):
    n = x.shape[0]
    slab = op(x.reshape(n // ROW_SLAB, ROW_SLAB, x.shape[1]), axis=0)
    return op(slab, axis=0, keepdims=True)


def _dsa_kernel(q_ref, qi_ref, smq_ref, smk_ref, k_ref, vt_ref, slope_ref, o_ref,
                kidx_ref, sc_ref, thr_ref, jcut_ref, wt_ref, q2_ref, qi2_ref, m_ref, l_ref, acc_ref, ot_ref,
                att_ref, p_ref, alpha_ref, *, n_buckets, topk):
    i = pl.program_id(1)

    @pl.when(i == 0)
    def _():
        kidx_ref[...] = smk_ref[...].astype(BF16)

    lane = lax.broadcasted_iota(jnp.int32, (Q_TILE, LANES), 1)
    low_half = lane < IDX_HEAD_DIM

    def head_pair_rows(x):
        zero = jnp.zeros_like(x)
        return jnp.concatenate([jnp.where(low_half, x, zero), jnp.where(low_half, zero, x)], axis=0)

    wt_ref[...] = jnp.transpose(smq_ref[...])[SM_W:SM_W + IDX_HEADS, :] * (IDX_HEADS ** -0.5 * IDX_HEAD_DIM ** -0.5)
    for pair in range(IDX_HEADS // 2):
        qi2_ref[pair] = head_pair_rows(qi_ref[:, pair * LANES:(pair + 1) * LANES])
    for pair in range(ATT_HEADS // 2):
        q2_ref[pair] = head_pair_rows(
            q_ref[:, pair * LANES:(pair + 1) * LANES] * jnp.asarray(ATT_HEAD_DIM ** -0.5, BF16))
    kf = float(topk)

    n_chunks = ((i + 1) * Q_TILE + KEY_CHUNK - 1) // KEY_CHUNK
    row_iota = lax.broadcasted_iota(jnp.int32, (KEY_CHUNK, Q_TILE), 0)
    t_pos = i * Q_TILE + lax.broadcasted_iota(jnp.int32, (KEY_CHUNK, Q_TILE), 1)
    vis_end = (t_pos // CHUNK + 1) * CHUNK

    def chunk_rows(c):
        return pl.ds(pl.multiple_of(c * KEY_CHUNK, KEY_CHUNK), KEY_CHUNK)

    def index_chunk(c, carry):
        rows = chunk_rows(c)
        score = jnp.zeros((KEY_CHUNK, Q_TILE), F32)
        for pair in range(IDX_HEADS // 2):
            logit = _nt_dot(kidx_ref[rows, :], qi2_ref[pair])
            score = (score + jnp.maximum(logit[:, 0:Q_TILE], 0.0) * wt_ref[2 * pair:2 * pair + 1, :]
                     + jnp.maximum(logit[:, Q_TILE:], 0.0) * wt_ref[2 * pair + 1:2 * pair + 2, :])
        sc_ref[rows, :] = jnp.where(c * KEY_CHUNK + row_iota < vis_end, score, -jnp.inf)
        return carry

    lax.fori_loop(0, n_chunks, index_chunk, 0)

    many = vis_end[0:1, :] > topk

    def select(n):
        s_pos = lax.broadcasted_iota(jnp.int32, (n, Q_TILE), 0)

        def count(pred):
            return _col_reduce(jnp.where(pred, 1.0, 0.0), jnp.sum)

        def bit_step(it, u):
            cand = u | (jnp.int32(1) << (31 - it))
            cand_f = pltpu.bitcast(jnp.where(cand < 0, cand ^ INT_MIN, ~cand), F32)
            c = count(sc_ref[0:n, :] >= cand_f)
            return jnp.where(c >= kf, cand, u)

        u_thr = lax.fori_loop(0, 32, bit_step, jnp.zeros((1, Q_TILE), jnp.int32))
        thr = pltpu.bitcast(jnp.where(u_thr < 0, u_thr ^ INT_MIN, ~u_thr), F32)
        thr = jnp.where(many, thr, -jnp.inf)
        thr_ref[...] = thr
        sc = sc_ref[0:n, :]
        gt = sc > thr
        eq = sc == thr
        need = kf - count(gt)
        excess = jnp.where(many, count(eq) - need, 0.0)
        jcut_ref[...] = jnp.where(many, n, -1)

        @pl.when(jnp.max(excess) > 0.0)
        def _():
            nbits = max(1, int(np.ceil(np.log2(n))))

            def idx_step(it, j0):
                cand = j0 | (jnp.int32(1) << (nbits - 1 - it))
                c = count((sc_ref[0:n, :] == thr) & (s_pos < cand))
                return jnp.where(c < need, cand, j0)

            j0 = lax.fori_loop(0, nbits, idx_step, jnp.zeros((1, Q_TILE), jnp.int32))
            jcut_ref[...] = jnp.where(many, j0, -1)

    for nb in range(1, n_buckets + 1):
        pl.when(n_chunks == nb)(functools.partial(select, nb * KEY_CHUNK))

    m_ref[...] = jnp.full(m_ref.shape, SOFTMAX_M0, F32)
    l_ref[...] = jnp.zeros(l_ref.shape, F32)
    acc_ref[...] = jnp.zeros(acc_ref.shape, F32)
    hd = ATT_HEAD_DIM

    def attend_chunk(c, carry):
        rows = chunk_rows(c)
        s_pos = c * KEY_CHUNK + row_iota
        sc = sc_ref[rows, :]
        sel = (sc > thr_ref[...]) | ((sc == thr_ref[...]) & (s_pos <= jcut_ref[...]))
        dsel = jnp.where(sel, jnp.abs(t_pos - s_pos).astype(F32), jnp.inf)
        for pair in range(ATT_HEADS // 2):
            att_ref[pair] = _nt_dot(k_ref[0, rows, pair * LANES:(pair + 1) * LANES], q2_ref[pair])
        for h in range(ATT_HEADS):
            pair, half = divmod(h, 2)
            cols = slice(half * Q_TILE, (half + 1) * Q_TILE)
            a = att_ref[pair, :, cols] - slope_ref[h:h + 1, :] * dsel
            m_old = m_ref[h:h + 1, :]
            m_new = jnp.maximum(m_old, _col_reduce(a, jnp.max))
            m_ref[h:h + 1, :] = m_new
            alpha_ref[h:h + 1, :] = jnp.exp2(m_old - m_new)
            p_ref[pair, :, cols] = jnp.exp2(a - m_new).astype(BF16)
        for pair in range(ATT_HEADS // 2):
            o2 = _dot(vt_ref[0, pair, :, rows], p_ref[pair])
            for half in range(2):
                h = 2 * pair + half
                cols = slice(half * Q_TILE, (half + 1) * Q_TILE)
                alpha = alpha_ref[h:h + 1, :]
                acc_ref[h] = acc_ref[h] * alpha + o2[half * hd:(half + 1) * hd, cols]
                l_ref[h:h + 1, :] = l_ref[h:h + 1, :] * alpha + o2[2 * hd:2 * hd + 1, cols]
        return carry

    lax.fori_loop(0, n_chunks, attend_chunk, 0)
    for h in range(ATT_HEADS):
        ot_ref[h * hd:(h + 1) * hd, :] = acc_ref[h] * (1.0 / l_ref[h:h + 1, :])
    o_ref[...] = jnp.transpose(ot_ref[...]).astype(BF16)


def _dsa(u_big, u_small, k_all, vt_all, b, s):
    t = b * s
    width = ATT_HEADS * ATT_HEAD_DIM
    nq = s // Q_TILE
    topk = min(TOPK_MAX, s // 4)
    assert s % KEY_CHUNK == 0
    slopes = 2.0 ** (-8.0 * jnp.arange(1, ATT_HEADS + 1, dtype=F32) / ATT_HEADS) * LOG2E
    slopes = jnp.broadcast_to(slopes[:, None], (ATT_HEADS, Q_TILE))
    qi_w = IDX_HEADS * IDX_HEAD_DIM
    return pl.pallas_call(
        functools.partial(_dsa_kernel, n_buckets=s // KEY_CHUNK, topk=topk),
        grid=(b, nq),
        in_specs=[
            pl.BlockSpec((Q_TILE, width), lambda bi, qi: (bi * nq + qi, COL_Q // width)),
            pl.BlockSpec((Q_TILE, qi_w), lambda bi, qi: (bi * nq + qi, COL_QI // qi_w)),
            pl.BlockSpec((Q_TILE, LANES), lambda bi, qi: (bi * nq + qi, 1)),
            pl.BlockSpec((s, LANES), lambda bi, qi: (bi, 0)),
            pl.BlockSpec((1, s, width), lambda bi, qi: (bi, 0, 0)),
            pl.BlockSpec((1, ATT_HEADS // 2, LANES + VT_ONES, s), lambda bi, qi: (bi, 0, 0, 0)),
            pl.BlockSpec((ATT_HEADS, Q_TILE), lambda bi, qi: (0, 0)),
        ],
        out_specs=pl.BlockSpec((Q_TILE, width), lambda bi, qi: (bi * nq + qi, 0)),
        out_shape=jax.ShapeDtypeStruct((t, width), BF16),
        scratch_shapes=[
            pltpu.VMEM((s, LANES), BF16),
            pltpu.VMEM((s, Q_TILE), F32),
            pltpu.VMEM((1, Q_TILE), F32),
            pltpu.VMEM((1, Q_TILE), jnp.int32),
            pltpu.VMEM((IDX_HEADS, Q_TILE), F32),
            pltpu.VMEM((ATT_HEADS // 2, 2 * Q_TILE, LANES), BF16),
            pltpu.VMEM((IDX_HEADS // 2, 2 * Q_TILE, LANES), BF16),
            pltpu.VMEM((ATT_HEADS, Q_TILE), F32),
            pltpu.VMEM((ATT_HEADS, Q_TILE), F32),
            pltpu.VMEM((ATT_HEADS, ATT_HEAD_DIM, Q_TILE), F32),
            pltpu.VMEM((width, Q_TILE), F32),
            pltpu.VMEM((ATT_HEADS // 2, KEY_CHUNK, 2 * Q_TILE), F32),
            pltpu.VMEM((ATT_HEADS // 2, KEY_CHUNK, 2 * Q_TILE), BF16),
            pltpu.VMEM((ATT_HEADS, Q_TILE), F32),
        ],
        compiler_params=pltpu.CompilerParams(
            dimension_semantics=("arbitrary", "arbitrary"), vmem_limit_bytes=VMEM_LIMIT),
        name="dsa",
    )(u_big, u_big, u_small, u_small, k_all, vt_all, slopes)


def _merge_kernel(x_ref, ys_ref, ya_ref, g0_ref, g1_ref, gb_ref, wbs_ref, wba_ref, wo_ref,
                  nf_ref, wr_ref, br_ref, x1_ref, h2_ref, gate_ref, gselc_ref, gselr_ref):
    def sig(v):
        return 1.0 / (1.0 + jnp.exp(-v))

    g0 = sig(g0_ref[...].astype(F32) + gb_ref[0:1, :])
    g1 = sig(g1_ref[...].astype(F32) + gb_ref[1:2, :])
    mixed = g0 * _dot(ys_ref[...], wbs_ref[...]) + g1 * _dot(ya_ref[...], wba_ref[...])
    x1 = x_ref[...] + _dot(mixed.astype(BF16), wo_ref[...])
    x1_ref[...] = x1
    h2 = x1 * lax.rsqrt(jnp.mean(x1 * x1, axis=-1, keepdims=True) + NORM_EPS) * nf_ref[...]
    h2b = h2.astype(BF16)
    h2_ref[...] = h2b

    n_e = MOE_GROUPS * MOE_EXPERTS
    logits = _dot(h2b, wr_ref[...]) + br_ref[...]
    lane = lax.broadcasted_iota(jnp.int32, logits.shape, 1)
    neg = -jnp.inf
    is_g = (lane >= n_e) & (lane < n_e + MOE_GROUPS)
    gl = jnp.where(is_g, logits, neg)
    gmax = jnp.max(gl, axis=-1, keepdims=True)
    g_val = 1.0 / jnp.sum(jnp.exp(gl - gmax), axis=-1, keepdims=True)
    g_sel = jnp.min(jnp.where(is_g & (gl == gmax), lane, 4 * LANES), axis=-1, keepdims=True) - n_e
    in_grp = (lane < n_e) & ((lane // MOE_EXPERTS) == g_sel)
    el = jnp.where(in_grp, logits, neg)
    m1 = jnp.max(el, axis=-1, keepdims=True)
    i1 = jnp.min(jnp.where(in_grp & (el == m1), lane, 4 * LANES), axis=-1, keepdims=True)
    el2 = jnp.where(lane == i1, neg, el)
    m2 = jnp.max(el2, axis=-1, keepdims=True)
    i2 = jnp.min(jnp.where(in_grp & (el2 == m2), lane, 4 * LANES), axis=-1, keepdims=True)
    p2 = jnp.exp(m2 - m1)
    inv = 1.0 / (1.0 + p2)
    gate = (jnp.where(lane == i1, inv, 0.0) + jnp.where(lane == i2, p2 * inv, 0.0)) * g_val
    gate8 = gate
    for g in range(1, MOE_GROUPS):
        gate8 = gate8 + pltpu.roll(gate, LANES - g * MOE_EXPERTS, 1)
    gate_ref[...] = jnp.where(lane < MOE_EXPERTS, gate8, 0.0)
    gsel = jnp.broadcast_to(g_sel.astype(F32), logits.shape)
    gselc_ref[...] = gsel
    gselr_ref[...] = jnp.transpose(gsel)[0:8, :]


def _merge(x2, y_ssm, y_att, u_big, gate_b, w_bs, w_ba, w_o, norm_ffn_g, w_r, b_r, tm):
    t, d = x2.shape
    d_in = y_ssm.shape[1]
    const = lambda shape: pl.BlockSpec(shape, lambda i: (0,) * len(shape))
    return pl.pallas_call(
        _merge_kernel,
        grid=(t // tm,),
        in_specs=[
            pl.BlockSpec((tm, d), lambda i: (i, 0)),
            pl.BlockSpec((tm, d_in), lambda i: (i, 0)),
            pl.BlockSpec((tm, d), lambda i: (i, 0)),
            pl.BlockSpec((tm, d), lambda i: (i, COL_G // d)),
            pl.BlockSpec((tm, d), lambda i: (i, COL_G // d + 1)),
            const((2, d)), const((d_in, d)), const((d, d)), const((d, d)), const((1, d)),
            const((d, LANES)), const((1, LANES)),
        ],
        out_specs=[
            pl.BlockSpec((tm, d), lambda i: (i, 0)),
            pl.BlockSpec((tm, d), lambda i: (i, 0)),
            pl.BlockSpec((tm, LANES), lambda i: (i, 0)),
            pl.BlockSpec((tm, LANES), lambda i: (i, 0)),
            pl.BlockSpec((8, tm), lambda i: (0, i)),
        ],
        out_shape=[
            jax.ShapeDtypeStruct((t, d), F32),
            jax.ShapeDtypeStruct((t, d), BF16),
            jax.ShapeDtypeStruct((t, LANES), F32),
            jax.ShapeDtypeStruct((t, LANES), F32),
            jax.ShapeDtypeStruct((8, t), F32),
        ],
        compiler_params=pltpu.CompilerParams(
            dimension_semantics=("arbitrary",), vmem_limit_bytes=VMEM_LIMIT),
        name="merge",
    )(x2, y_ssm, y_att, u_big, u_big, gate_b, w_bs, w_ba, w_o, norm_ffn_g, w_r, b_r)


MOE_SUB = 256
MOE_ROWS = 128
MOE_HALVES = 2


def _moe_kernel(h_ref, gate_ref, gselc_ref, gselr_ref, x1_ref, w1_ref, w3_ref, w2_ref, e_ref,
                triu_ref, tril_ref, nf_ref, o_ref, xc_ref, gc_ref, yc_ref, off_ref, cnt_ref):
    i, g, half = pl.program_id(0), pl.program_id(1), pl.program_id(2)
    tt = h_ref.shape[0]
    n_sub = tt // MOE_SUB
    g_f = g.astype(F32)

    @pl.when((i == 0) & (g == 0) & (half == 0))
    def _():
        xc_ref[...] = jnp.zeros(xc_ref.shape, BF16)
        gc_ref[...] = jnp.zeros(gc_ref.shape, F32)
        yc_ref[...] = jnp.zeros(yc_ref.shape, F32)

    @pl.when((g == 0) & (half == 0))
    def _():
        o_ref[...] = x1_ref[...]

    @pl.when(half == 0)
    def _():
        off = jnp.int32(0)
        for s in range(n_sub):
            tok = slice(s * MOE_SUB, (s + 1) * MOE_SUB)
            member = gselr_ref[:, tok] == g_f
            m_f = jnp.where(member, 1.0, 0.0)
            rank = _dot(m_f.astype(BF16), triu_ref[...])
            cnt = jnp.sum(m_f[0:1, :]).astype(jnp.int32)
            off_ref[s] = off
            cnt_ref[s] = cnt
            g8 = gate_ref[tok, :]
            g_hi = g8.astype(BF16)
            g_lo = (g8 - g_hi.astype(F32)).astype(BF16)

            rank = jnp.where(member, rank, -1.0)

            def compact(rbase, off=off, tok=tok, rank=rank, g_hi=g_hi, g_lo=g_lo):
                r = (rbase + lax.broadcasted_iota(jnp.int32, (MOE_ROWS, MOE_SUB), 0)).astype(F32)
                onehot = jnp.where(r == rank[0:1, :], 1.0, 0.0).astype(BF16)
                rows16 = pl.ds(pl.multiple_of(off + rbase, 16), MOE_ROWS)
                xc_ref[rows16, :] = _dot(onehot, h_ref[tok, :]).astype(BF16)
                gc_ref[rows16, :] = _dot(onehot, g_hi) + _dot(onehot, g_lo)

            compact(0)
            pl.when(cnt > MOE_ROWS)(functools.partial(compact, MOE_ROWS))
            off = off + ((cnt + 15) // 16) * 16
        off_ref[n_sub] = off

    def ffn_block(blk, carry):
        rows = pl.ds(pl.multiple_of(blk * MOE_ROWS, MOE_ROWS), MOE_ROWS)
        x = xc_ref[rows, :]
        a = _dot(x, w1_ref[0, 0])
        b3 = _dot(x, w3_ref[0, 0])
        gx = gc_ref[rows, :]
        gx_hi = gx.astype(BF16)
        gx_lo = (gx - gx_hi.astype(F32)).astype(BF16)
        gexp = _dot(gx_hi, e_ref[0]) + _dot(gx_lo, e_ref[0])
        y = _dot((_silu(a) * b3 * gexp).astype(BF16), w2_ref[0, 0])

        @pl.when(half == 0)
        def _():
            yc_ref[rows, :] = y

        @pl.when(half != 0)
        def _():
            yc_ref[rows, :] += y

        return carry

    lax.fori_loop(0, (off_ref[n_sub] + MOE_ROWS - 1) // MOE_ROWS, ffn_block, 0)

    @pl.when(half == MOE_HALVES - 1)
    def _():
        for s in range(n_sub):
            tok = slice(s * MOE_SUB, (s + 1) * MOE_SUB)
            member = gselc_ref[tok, :] == g_f
            rank = _dot(tril_ref[...], jnp.where(member, 1.0, 0.0).astype(BF16))
            rank = jnp.where(member, rank, -1.0)
            off = off_ref[s]

            def scatter(rbase, off=off, tok=tok, rank=rank):
                r = (rbase + lax.broadcasted_iota(jnp.int32, (MOE_SUB, MOE_ROWS), 1)).astype(F32)
                onehot_t = jnp.where(r == rank, 1.0, 0.0).astype(BF16)
                rows8 = pl.ds(pl.multiple_of(off + rbase, 8), MOE_ROWS)
                o_ref[tok, :] += _dot(onehot_t, yc_ref[rows8, :].astype(BF16))

            scatter(0)
            pl.when(cnt_ref[s] > MOE_ROWS)(functools.partial(scatter, MOE_ROWS))

    @pl.when((g == pl.num_programs(1) - 1) & (half == MOE_HALVES - 1))
    def _():
        x2 = o_ref[...]
        o_ref[...] = x2 * lax.rsqrt(jnp.mean(x2 * x2, axis=-1, keepdims=True) + NORM_EPS) * nf_ref[...]


def _moe(h2, gate8, gselc, gselr, x1, w1, w3, w2, norm_final_g, tt):
    t, d = x1.shape
    n_g, n_e, _, ff = w1.shape
    eh = n_e // MOE_HALVES
    assert MOE_ROWS == LANES and tt % MOE_SUB == 0 and n_e <= 8
    w1h = jnp.transpose(w1.reshape(n_g, MOE_HALVES, eh, d, ff), (0, 1, 3, 2, 4)).reshape(n_g, MOE_HALVES, d, eh * ff)
    w3h = jnp.transpose(w3.reshape(n_g, MOE_HALVES, eh, d, ff), (0, 1, 3, 2, 4)).reshape(n_g, MOE_HALVES, d, eh * ff)
    w2h = w2.reshape(n_g, MOE_HALVES, eh * ff, d)
    expand = np.zeros((MOE_HALVES, LANES, eh * ff), np.float32)
    for hf in range(MOE_HALVES):
        for e in range(eh):
            expand[hf, hf * eh + e, e * ff:(e + 1) * ff] = 1.0
    triu = np.triu(np.ones((MOE_SUB, MOE_SUB), np.float32), 1)
    xrows = tt + 2 * MOE_ROWS
    const = lambda shape: pl.BlockSpec(shape, lambda i, g, hf: (0,) * len(shape))
    tok = lambda width: pl.BlockSpec((tt, width), lambda i, g, hf: (i, 0))
    wspec = pl.BlockSpec((1, 1, d, eh * ff), lambda i, g, hf: (g, hf, 0, 0))
    return pl.pallas_call(
        _moe_kernel,
        grid=(t // tt, n_g, MOE_HALVES),
        in_specs=[
            tok(d), tok(LANES), tok(LANES),
            pl.BlockSpec((8, tt), lambda i, g, hf: (0, i)),
            tok(d),
            wspec, wspec,
            pl.BlockSpec((1, 1, eh * ff, d), lambda i, g, hf: (g, hf, 0, 0)),
            pl.BlockSpec((1, LANES, eh * ff), lambda i, g, hf: (hf, 0, 0)),
            const((MOE_SUB, MOE_SUB)), const((MOE_SUB, MOE_SUB)), const((1, d)),
        ],
        out_specs=pl.BlockSpec((tt, d), lambda i, g, hf: (i, 0)),
        out_shape=jax.ShapeDtypeStruct((t, d), F32),
        scratch_shapes=[
            pltpu.VMEM((xrows, d), BF16),
            pltpu.VMEM((xrows, LANES), F32),
            pltpu.VMEM((xrows, d), F32),
            pltpu.SMEM((8,), jnp.int32),
            pltpu.SMEM((8,), jnp.int32),
        ],
        compiler_params=pltpu.CompilerParams(
            dimension_semantics=("arbitrary", "arbitrary", "arbitrary"), vmem_limit_bytes=VMEM_LIMIT),
        name="moe",
    )(h2, gate8, gselc, gselr, x1, w1h, w3h, w2h, jnp.asarray(expand, BF16),
      jnp.asarray(triu, BF16), jnp.asarray(triu.T, BF16), norm_final_g)


def _pick_tile(t, pref):
    tm = pref
    while t % tm:
        tm //= 2
    return tm


def kernel(x, norm_mix_g, w_in, conv_w, conv_b, dt_bias, a_log, d_skip, ssm_norm_g, kv_norm_g, w_uk, w_uv,
           gate_b, w_branch_ssm, w_branch_att, w_out, norm_ffn_g, w_group, b_group, w_router, b_router,
           w_e1, w_e3, w_e2, norm_final_g):
    b, s, d = x.shape
    depth = w_in.shape[0]
    t = b * s
    d_in = 2 * d
    heads = d_in // SSM_HEAD_DIM
    n_bc = 2 * SSM_GROUPS * SSM_STATE
    width = ATT_HEADS * ATT_HEAD_DIM
    qi_w = IDX_HEADS * IDX_HEAD_DIM
    assert (d, d_in, n_bc, width) == (1024, COL_XS - COL_Z, COL_Q - COL_BC, COL_QI - COL_Q)
    assert s % (2 * CHUNK) == 0 and s % Q_TILE == 0

    assert depth == 1, "the final RMSNorm is fused into the expert kernel of the only layer"
    layer = 0
    x2 = x.reshape(t, d)

    sizes = (d_in, d_in + n_bc, heads, width, KV_LATENT, qi_w, IDX_HEAD_DIM, IDX_HEADS, 2 * d)
    cuts = np.concatenate([[0], np.cumsum(sizes)])
    seg = lambda k: w_in[layer][:, int(cuts[k]):int(cuts[k + 1])]
    w_z, w_xbc, w_dt, w_q, w_ckv, w_qi, w_ki, w_wi, w_g = (seg(k) for k in range(9))
    zpad = lambda n: jnp.zeros((d, n), w_in.dtype)
    w_big = jnp.concatenate(
        [w_z, w_xbc, w_q, w_qi, w_ckv, zpad(COL_G - COL_CKV - KV_LATENT), w_g], axis=1).astype(BF16)
    w_small = jnp.concatenate(
        [w_ki, w_ki, w_dt, w_wi, zpad(N_SMALL - 2 * IDX_HEAD_DIM - heads - IDX_HEADS)], axis=1).astype(BF16)
    assert w_big.shape[1] == N_BIG and w_small.shape[1] == N_SMALL

    u_big, u_small = _in_proj(x2, norm_mix_g[layer][None, :], w_big, w_small, _pick_tile(t, 1024), 1536)

    y_ssm = _ssd(u_big, u_small, conv_w[layer], conv_b[layer], dt_bias[layer], a_log[layer],
                 d_skip[layer], ssm_norm_g[layer], b, s, 2)

    w_k = jnp.transpose(w_uk[layer], (1, 0, 2)).reshape(KV_LATENT, width).astype(BF16)
    w_vt = jnp.transpose(w_uv[layer], (0, 2, 1)).reshape(width, KV_LATENT).astype(BF16)
    k_all, vt_all = _kv_proj(u_big, kv_norm_g[layer][None, :], w_k, w_vt, b, s)
    y_att = _dsa(u_big, u_small, k_all, vt_all, b, s)

    n_e = MOE_GROUPS * MOE_EXPERTS
    w_r = jnp.concatenate(
        [jnp.transpose(w_router[layer], (1, 0, 2)).reshape(d, n_e), w_group[layer],
         jnp.zeros((d, LANES - n_e - MOE_GROUPS), F32)], axis=1).astype(BF16)
    b_r = jnp.concatenate(
        [b_router[layer].reshape(n_e), b_group[layer], jnp.zeros((LANES - n_e - MOE_GROUPS,), F32)])[None, :]
    x1, h2, gate8, gselc, gselr = _merge(
        x2, y_ssm, y_att, u_big, gate_b[layer],
        w_branch_ssm[layer].astype(BF16), w_branch_att[layer].astype(BF16),
        w_out[layer].astype(BF16), norm_ffn_g[layer][None, :], w_r, b_r, _pick_tile(t, 512))

    out = _moe(h2, gate8, gselc, gselr, x1, w_e1[layer].astype(BF16), w_e3[layer].astype(BF16),
               w_e2[layer].astype(BF16), norm_final_g[None, :], _pick_tile(t, 1024))
    return out.reshape(b, s, d)
```

```python
import functools

import numpy as np
import jax
import jax.numpy as jnp
from jax import lax
from jax.experimental import pallas as pl
from jax.experimental.pallas import tpu as pltpu

F32 = jnp.float32
BF16 = jnp.bfloat16

NORM_EPS = 1e-6
CHUNK = 64
SSM_HEAD_DIM = 64
SSM_GROUPS = 4
SSM_STATE = 128
SSM_CONV = 4
ATT_HEADS = 16
ATT_HEAD_DIM = 64
KV_LATENT = 256
IDX_HEADS = 8
IDX_HEAD_DIM = 64
TOPK_MAX = 256
MOE_GROUPS = 4
MOE_EXPERTS = 8
MOE_FF = 256

LANES = 128
Q_TILE = 128
CONV_PAD = 16
VMEM_LIMIT = 56 * 1024 * 1024
INT_MIN = -(2 ** 31)
LOG2E = 1.4426950408889634
ROW_SLAB = 64
KEY_CHUNK = 256
VT_ONES = 16
SOFTMAX_M0 = -1e30

COL_Z, COL_XS, COL_BC, COL_Q, COL_QI, COL_CKV, COL_G = 0, 2048, 4096, 5120, 6144, 6656, 7168
N_BIG = 9216
N_SMALL = 256
SM_DT, SM_W = 0, 32


def _nt_dot(a, b):
    return lax.dot_general(a, b, (((1,), (1,)), ((), ())), preferred_element_type=F32)


def _dot(a, b):
    return jnp.dot(a, b, preferred_element_type=F32)


def _split3(a):
    hi = a.astype(BF16)
    r = a - hi.astype(F32)
    mid = r.astype(BF16)
    lo = (r - mid.astype(F32)).astype(BF16)
    return hi, mid, lo


def _silu(v):
    hv = 0.5 * v
    return hv + hv * jnp.tanh(hv)


def _in_proj_kernel(x_ref, g_ref, wb_ref, ws_ref, ub_ref, us_ref, h_ref):
    j = pl.program_id(1)

    @pl.when(j == 0)
    def _():
        xf = x_ref[...]
        y = xf * lax.rsqrt(jnp.mean(xf * xf, axis=-1, keepdims=True) + NORM_EPS)
        h = (y * g_ref[...]).astype(BF16)
        h_ref[...] = h
        us_ref[...] = _dot(h, ws_ref[...])

    ub_ref[...] = _dot(h_ref[...], wb_ref[...]).astype(BF16)


def _in_proj(x2, g, w_big, w_small, tm, tn):
    t, d = x2.shape
    return pl.pallas_call(
        _in_proj_kernel,
        grid=(t // tm, N_BIG // tn),
        in_specs=[
            pl.BlockSpec((tm, d), lambda i, j: (i, 0)),
            pl.BlockSpec((1, d), lambda i, j: (0, 0)),
            pl.BlockSpec((d, tn), lambda i, j: (0, j)),
            pl.BlockSpec((d, N_SMALL), lambda i, j: (0, 0)),
        ],
        out_specs=[
            pl.BlockSpec((tm, tn), lambda i, j: (i, j)),
            pl.BlockSpec((tm, N_SMALL), lambda i, j: (i, 0)),
        ],
        out_shape=[
            jax.ShapeDtypeStruct((t, N_BIG), BF16),
            jax.ShapeDtypeStruct((t, N_SMALL), F32),
        ],
        scratch_shapes=[pltpu.VMEM((tm, d), BF16)],
        compiler_params=pltpu.CompilerParams(
            dimension_semantics=("arbitrary", "arbitrary"), vmem_limit_bytes=VMEM_LIMIT),
        name="in_proj",
    )(x2, g, w_big, w_small)


def _ssd_kernel(z_ref, xs_ref, bc_ref, sm_ref, cwx_ref, cwb_ref, cbx_ref, cbb_ref,
                dtb_ref, alog_ref, dsk_ref, ng_ref, e3_ref, tri_ref, it_ref, cm_ref, bm_ref, shift_ref,
                y_ref, xpx_ref, xpb_ref, st_ref, *, nch):
    lb = nch * CHUNK
    d_in = xs_ref.shape[-1]
    heads = d_in // SSM_HEAD_DIM
    gw = d_in // SSM_GROUPS
    gn = SSM_STATE

    @pl.when(pl.program_id(1) == 0)
    def _():
        xpx_ref[0:CONV_PAD, :] = jnp.zeros((CONV_PAD, d_in), BF16)
        xpb_ref[0:CONV_PAD, :] = jnp.zeros((CONV_PAD, xpb_ref.shape[-1]), BF16)
        st_ref[...] = jnp.zeros(st_ref.shape, F32)

    def conv(src_ref, pad_ref, w_ref, b_ref):
        cur = src_ref[...]
        pad_ref[CONV_PAD:CONV_PAD + lb, :] = cur
        delayed = _dot(shift_ref[...], pad_ref[...])
        acc = b_ref[...] + w_ref[SSM_CONV - 1:SSM_CONV, :] * cur.astype(F32)
        for k in range(SSM_CONV - 1):
            acc = acc + w_ref[k:k + 1, :] * delayed[k * lb:(k + 1) * lb, :]
        pad_ref[0:CONV_PAD, :] = pad_ref[lb:lb + CONV_PAD, :]
        return _silu(acc)

    xs = conv(xs_ref, xpx_ref, cwx_ref, cbx_ref)
    bc = conv(bc_ref, xpb_ref, cwb_ref, cbb_ref)

    dt_in = sm_ref[:, SM_DT:SM_DT + heads] + dtb_ref[...]
    dt = jnp.maximum(dt_in, 0.0) + jnp.log1p(jnp.exp(-jnp.abs(dt_in)))
    da = dt * (-jnp.exp(alog_ref[...]))
    tri = tri_ref[...]
    hi, mid, lo = _split3(da)
    a_cs = _dot(tri, hi) + _dot(tri, mid) + _dot(tri, lo)
    a_last = jnp.concatenate(
        [jnp.broadcast_to(a_cs[c * CHUNK + CHUNK - 1:c * CHUNK + CHUNK, :], (CHUNK, heads))
         for c in range(nch)], axis=0)
    stacked = jnp.concatenate([a_cs, jnp.exp(a_cs), jnp.exp(a_last - a_cs), dt], axis=0)
    s_hi, s_mid, s_lo = _split3(stacked)
    ex = _dot(jnp.concatenate([s_hi, s_mid, s_lo], axis=1), e3_ref[...])

    it_mask = it_ref[...] > 0.0
    cm_mask = cm_ref[...] > 0.0
    bm = bm_ref[...]
    it_bf = it_ref[...].astype(BF16)
    zf = z_ref[...].astype(F32)
    zg = _silu(zf)

    for c in range(nch):
        r0 = c * CHUNK
        a_l = ex[r0:r0 + CHUNK]
        exp_l = ex[lb + r0:lb + r0 + CHUNK]
        dst = ex[2 * lb + r0:2 * lb + r0 + CHUNK]
        dtx = ex[3 * lb + r0:3 * lb + r0 + CHUNK]
        xs_c = xs[r0:r0 + CHUNK]
        xdt = xs_c * dtx
        a_s = jnp.sum(jnp.where(it_mask, a_l, 0.0), axis=0, keepdims=True)
        decay = jnp.exp(jnp.where(cm_mask, a_l - a_s, -jnp.inf))
        xst = (xdt * dst).astype(BF16)
        xdt_bf = xdt.astype(BF16)
        y_parts = []
        for g in range(SSM_GROUPS):
            b_g = bc[r0:r0 + CHUNK, g * gn:(g + 1) * gn]
            c_g = bc[r0:r0 + CHUNK, (SSM_GROUPS + g) * gn:(SSM_GROUPS + g + 1) * gn]
            b_bf = b_g.astype(BF16)
            c_bf = c_g.astype(BF16)
            cb = _nt_dot(c_bf, b_bf)
            cbx = _dot(cb.astype(BF16), it_bf[:, 0:gw])
            w = (cbx * decay[:, g * gw:(g + 1) * gw]).astype(BF16)
            yd = []
            for jt in range(gw // 256):
                lo_c = g * gw + jt * 256
                xt = xdt_bf[:, lo_c:lo_c + 256]
                bd = jnp.concatenate([xt] * 4, axis=0) * bm
                yd.append(_dot(w[:, jt * 256:(jt + 1) * 256], bd))
            y_diag = jnp.concatenate(yd, axis=1)
            st_g = st_ref[g]
            y_off = _dot(c_bf, st_g.astype(BF16)) * exp_l[:, g * gw:(g + 1) * gw]
            new_st = _dot(jnp.transpose(b_g).astype(BF16), xst[:, g * gw:(g + 1) * gw])
            st_ref[g] = st_g * exp_l[CHUNK - 1:CHUNK, g * gw:(g + 1) * gw] + new_st
            y_g = y_diag + y_off + dsk_ref[:, g * gw:(g + 1) * gw] * xs_c[:, g * gw:(g + 1) * gw]
            y_g = y_g * zg[r0:r0 + CHUNK, g * gw:(g + 1) * gw]
            y_g = y_g * lax.rsqrt(jnp.mean(y_g * y_g, axis=-1, keepdims=True) + NORM_EPS)
            y_parts.append(y_g * ng_ref[:, g * gw:(g + 1) * gw])
        y_ref[r0:r0 + CHUNK, :] = jnp.concatenate(y_parts, axis=1).astype(BF16)


def _ssd(u_big, u_small, conv_w, conv_b, dt_bias, a_log, d_skip, ssm_norm_g, b, s, nch):
    t = b * s
    d_in = COL_BC - COL_XS
    n_bc = COL_Q - COL_BC
    heads = d_in // SSM_HEAD_DIM
    lb = nch * CHUNK
    nblk = s // lb
    assert d_in // SSM_GROUPS == 512 and SSM_HEAD_DIM == CHUNK

    e = np.kron(np.eye(heads, dtype=np.float32), np.ones((1, SSM_HEAD_DIM), np.float32))
    e3 = jnp.asarray(np.concatenate([e, e, e], axis=0), BF16)
    tri = jnp.asarray(np.kron(np.eye(nch, dtype=np.float32), np.tril(np.ones((CHUNK, CHUNK), np.float32))), BF16)
    it = jnp.asarray(np.tile(np.eye(CHUNK, dtype=np.float32), (1, heads)))
    cm = jnp.asarray(np.tile(np.tril(np.ones((CHUNK, CHUNK), np.float32)), (1, heads)))
    bm = jnp.asarray(np.kron(np.eye(4, dtype=np.float32), np.ones((64, 64), np.float32)), BF16)
    shift = np.zeros(((SSM_CONV - 1) * lb, CONV_PAD + lb), np.float32)
    for k in range(SSM_CONV - 1):
        shift[k * lb + np.arange(lb), CONV_PAD + np.arange(lb) - (SSM_CONV - 1 - k)] = 1.0
    shift = jnp.asarray(shift, BF16)

    cwx, cwb = conv_w[:, :d_in], conv_w[:, d_in:]
    cbx, cbb = conv_b[None, :d_in], conv_b[None, d_in:]
    dsk = jnp.repeat(d_skip, SSM_HEAD_DIM)[None, :]
    const = lambda shape: pl.BlockSpec(shape, lambda bi, ci: (0,) * len(shape))
    row = lambda width, col: pl.BlockSpec((lb, width), lambda bi, ci: (bi * nblk + ci, col))

    return pl.pallas_call(
        functools.partial(_ssd_kernel, nch=nch),
        grid=(b, nblk),
        in_specs=[
            row(d_in, COL_Z // d_in), row(d_in, COL_XS // d_in), row(n_bc, COL_BC // n_bc),
            row(LANES, 1),
            const((SSM_CONV, d_in)), const((SSM_CONV, n_bc)), const((1, d_in)), const((1, n_bc)),
            const((1, heads)), const((1, heads)), const((1, d_in)), const((1, d_in)),
            const((3 * heads, d_in)), const((lb, lb)), const((CHUNK, d_in)), const((CHUNK, d_in)),
            const((256, 256)), const(((SSM_CONV - 1) * lb, CONV_PAD + lb)),
        ],
        out_specs=pl.BlockSpec((lb, d_in), lambda bi, ci: (bi * nblk + ci, 0)),
        out_shape=jax.ShapeDtypeStruct((t, d_in), BF16),
        scratch_shapes=[
            pltpu.VMEM((lb + CONV_PAD, d_in), BF16),
            pltpu.VMEM((lb + CONV_PAD, n_bc), BF16),
            pltpu.VMEM((SSM_GROUPS, SSM_STATE, d_in // SSM_GROUPS), F32),
        ],
        compiler_params=pltpu.CompilerParams(
            dimension_semantics=("arbitrary", "arbitrary"), vmem_limit_bytes=VMEM_LIMIT),
        name="ssd",
    )(u_big, u_big, u_big, u_small, cwx, cwb, cbx, cbb, dt_bias[None, :], a_log[None, :], dsk,
      ssm_norm_g[None, :], e3, tri, it, cm, bm, shift)


def _kv_proj_kernel(c_ref, g_ref, wk_ref, wvt_ref, k_ref, vt_ref):
    cf = c_ref[...].astype(F32)
    y = cf * lax.rsqrt(jnp.mean(cf * cf, axis=-1, keepdims=True) + NORM_EPS)
    cn = (y * g_ref[...]).astype(BF16)
    k_ref[0] = (_dot(cn, wk_ref[...]) * LOG2E).astype(BF16)
    vt = _nt_dot(wvt_ref[...], cn).astype(BF16)
    for pair in range(ATT_HEADS // 2):
        vt_ref[0, pair, 0:LANES, :] = vt[pair * LANES:(pair + 1) * LANES, :]
        vt_ref[0, pair, LANES:, :] = jnp.ones((VT_ONES, vt.shape[1]), BF16)


def _kv_proj(u_big, kv_norm_g, w_k, w_vt, b, s):
    width = ATT_HEADS * ATT_HEAD_DIM
    return pl.pallas_call(
        _kv_proj_kernel,
        grid=(b,),
        in_specs=[
            pl.BlockSpec((s, KV_LATENT), lambda bi: (bi, COL_CKV // KV_LATENT)),
            pl.BlockSpec((1, KV_LATENT), lambda bi: (0, 0)),
            pl.BlockSpec((KV_LATENT, width), lambda bi: (0, 0)),
            pl.BlockSpec((width, KV_LATENT), lambda bi: (0, 0)),
        ],
        out_specs=[
            pl.BlockSpec((1, s, width), lambda bi: (bi, 0, 0)),
            pl.BlockSpec((1, ATT_HEADS // 2, LANES + VT_ONES, s), lambda bi: (bi, 0, 0, 0)),
        ],
        out_shape=[
            jax.ShapeDtypeStruct((b, s, width), BF16),
            jax.ShapeDtypeStruct((b, ATT_HEADS // 2, LANES + VT_ONES, s), BF16),
        ],
        compiler_params=pltpu.CompilerParams(
            dimension_semantics=("arbitrary",), vmem_limit_bytes=VMEM_LIMIT),
        name="kv_proj",
    )(u_big, kv_norm_g, w_k, w_vt)


def _col_reduce(x, op):
    n = x.shape[0]
    slab = op(x.reshape(n // ROW_SLAB, ROW_SLAB, x.shape[1]), axis=0)
    return op(slab, axis=0, keepdims=True)


def _dsa_kernel(q_ref, qi_ref, smq_ref, smk_ref, k_ref, vt_ref, slope_ref, o_ref,
                kidx_ref, sc_ref, thr_ref, jcut_ref, wt_ref, q2_ref, qi2_ref, m_ref, l_ref, acc_ref, ot_ref,
                att_ref, p_ref, alpha_ref, *, n_buckets, topk):
    i = pl.program_id(1)

    @pl.when(i == 0)
    def _():
        kidx_ref[...] = smk_ref[...].astype(BF16)

    lane = lax.broadcasted_iota(jnp.int32, (Q_TILE, LANES), 1)
    low_half = lane < IDX_HEAD_DIM

    def head_pair_rows(x):
        zero = jnp.zeros_like(x)
        return jnp.concatenate([jnp.where(low_half, x, zero), jnp.where(low_half, zero, x)], axis=0)

    wt_ref[...] = jnp.transpose(smq_ref[...])[SM_W:SM_W + IDX_HEADS, :] * (IDX_HEADS ** -0.5 * IDX_HEAD_DIM ** -0.5)
    for pair in range(IDX_HEADS // 2):
        qi2_ref[pair] = head_pair_rows(qi_ref[:, pair * LANES:(pair + 1) * LANES])
    for pair in range(ATT_HEADS // 2):
        q2_ref[pair] = head_pair_rows(
            q_ref[:, pair * LANES:(pair + 1) * LANES] * jnp.asarray(ATT_HEAD_DIM ** -0.5, BF16))
    kf = float(topk)

    n_chunks = ((i + 1) * Q_TILE + KEY_CHUNK - 1) // KEY_CHUNK
    row_iota = lax.broadcasted_iota(jnp.int32, (KEY_CHUNK, Q_TILE), 0)
    t_pos = i * Q_TILE + lax.broadcasted_iota(jnp.int32, (KEY_CHUNK, Q_TILE), 1)
    vis_end = (t_pos // CHUNK + 1) * CHUNK

    def chunk_rows(c):
        return pl.ds(pl.multiple_of(c * KEY_CHUNK, KEY_CHUNK), KEY_CHUNK)

    def index_chunk(c, carry):
        rows = chunk_rows(c)
        score = jnp.zeros((KEY_CHUNK, Q_TILE), F32)
        for pair in range(IDX_HEADS // 2):
            logit = _nt_dot(kidx_ref[rows, :], qi2_ref[pair])
            score = (score + jnp.maximum(logit[:, 0:Q_TILE], 0.0) * wt_ref[2 * pair:2 * pair + 1, :]
                     + jnp.maximum(logit[:, Q_TILE:], 0.0) * wt_ref[2 * pair + 1:2 * pair + 2, :])
        sc_ref[rows, :] = jnp.where(c * KEY_CHUNK + row_iota < vis_end, score, -jnp.inf)
        return carry

    lax.fori_loop(0, n_chunks, index_chunk, 0)

    many = vis_end[0:1, :] > topk

    def select(n):
        s_pos = lax.broadcasted_iota(jnp.int32, (n, Q_TILE), 0)

        def count(pred):
            return _col_reduce(jnp.where(pred, 1.0, 0.0), jnp.sum)

        def bit_step(it, u):
            cand = u | (jnp.int32(1) << (31 - it))
            cand_f = pltpu.bitcast(jnp.where(cand < 0, cand ^ INT_MIN, ~cand), F32)
            c = count(sc_ref[0:n, :] >= cand_f)
            return jnp.where(c >= kf, cand, u)

        u_thr = lax.fori_loop(0, 32, bit_step, jnp.zeros((1, Q_TILE), jnp.int32))
        thr = pltpu.bitcast(jnp.where(u_thr < 0, u_thr ^ INT_MIN, ~u_thr), F32)
        thr = jnp.where(many, thr, -jnp.inf)
        thr_ref[...] = thr
        sc = sc_ref[0:n, :]
        gt = sc > thr
        eq = sc == thr
        need = kf - count(gt)
        excess = jnp.where(many, count(eq) - need, 0.0)
        jcut_ref[...] = jnp.where(many, n, -1)

        @pl.when(jnp.max(excess) > 0.0)
        def _():
            nbits = max(1, int(np.ceil(np.log2(n))))

            def idx_step(it, j0):
                cand = j0 | (jnp.int32(1) << (nbits - 1 - it))
                c = count((sc_ref[0:n, :] == thr) & (s_pos < cand))
                return jnp.where(c < need, cand, j0)

            j0 = lax.fori_loop(0, nbits, idx_step, jnp.zeros((1, Q_TILE), jnp.int32))
            jcut_ref[...] = jnp.where(many, j0, -1)

    for nb in range(1, n_buckets + 1):
        pl.when(n_chunks == nb)(functools.partial(select, nb * KEY_CHUNK))

    m_ref[...] = jnp.full(m_ref.shape, SOFTMAX_M0, F32)
    l_ref[...] = jnp.zeros(l_ref.shape, F32)
    acc_ref[...] = jnp.zeros(acc_ref.shape, F32)
    hd = ATT_HEAD_DIM

    def attend_chunk(c, carry):
        rows = chunk_rows(c)
        s_pos = c * KEY_CHUNK + row_iota
        sc = sc_ref[rows, :]
        sel = (sc > thr_ref[...]) | ((sc == thr_ref[...]) & (s_pos <= jcut_ref[...]))
        dsel = jnp.where(sel, jnp.abs(t_pos - s_pos).astype(F32), jnp.inf)
        for pair in range(ATT_HEADS // 2):
            att_ref[pair] = _nt_dot(k_ref[0, rows, pair * LANES:(pair + 1) * LANES], q2_ref[pair])
        for h in range(ATT_HEADS):
            pair, half = divmod(h, 2)
            cols = slice(half * Q_TILE, (half + 1) * Q_TILE)
            a = att_ref[pair, :, cols] - slope_ref[h:h + 1, :] * dsel
            m_old = m_ref[h:h + 1, :]
            m_new = jnp.maximum(m_old, _col_reduce(a, jnp.max))
            m_ref[h:h + 1, :] = m_new
            alpha_ref[h:h + 1, :] = jnp.exp2(m_old - m_new)
            p_ref[pair, :, cols] = jnp.exp2(a - m_new).astype(BF16)
        for pair in range(ATT_HEADS // 2):
            o2 = _dot(vt_ref[0, pair, :, rows], p_ref[pair])
            for half in range(2):
                h = 2 * pair + half
                cols = slice(half * Q_TILE, (half + 1) * Q_TILE)
                alpha = alpha_ref[h:h + 1, :]
                acc_ref[h] = acc_ref[h] * alpha + o2[half * hd:(half + 1) * hd, cols]
                l_ref[h:h + 1, :] = l_ref[h:h + 1, :] * alpha + o2[2 * hd:2 * hd + 1, cols]
        return carry

    lax.fori_loop(0, n_chunks, attend_chunk, 0)
    for h in range(ATT_HEADS):
        ot_ref[h * hd:(h + 1) * hd, :] = acc_ref[h] * (1.0 / l_ref[h:h + 1, :])
    o_ref[...] = jnp.transpose(ot_ref[...]).astype(BF16)


def _dsa(u_big, u_small, k_all, vt_all, b, s):
    t = b * s
    width = ATT_HEADS * ATT_HEAD_DIM
    nq = s // Q_TILE
    topk = min(TOPK_MAX, s // 4)
    assert s % KEY_CHUNK == 0
    slopes = 2.0 ** (-8.0 * jnp.arange(1, ATT_HEADS + 1, dtype=F32) / ATT_HEADS) * LOG2E
    slopes = jnp.broadcast_to(slopes[:, None], (ATT_HEADS, Q_TILE))
    qi_w = IDX_HEADS * IDX_HEAD_DIM
    return pl.pallas_call(
        functools.partial(_dsa_kernel, n_buckets=s // KEY_CHUNK, topk=topk),
        grid=(b, nq),
        in_specs=[
            pl.BlockSpec((Q_TILE, width), lambda bi, qi: (bi * nq + qi, COL_Q // width)),
            pl.BlockSpec((Q_TILE, qi_w), lambda bi, qi: (bi * nq + qi, COL_QI // qi_w)),
            pl.BlockSpec((Q_TILE, LANES), lambda bi, qi: (bi * nq + qi, 1)),
            pl.BlockSpec((s, LANES), lambda bi, qi: (bi, 0)),
            pl.BlockSpec((1, s, width), lambda bi, qi: (bi, 0, 0)),
            pl.BlockSpec((1, ATT_HEADS // 2, LANES + VT_ONES, s), lambda bi, qi: (bi, 0, 0, 0)),
            pl.BlockSpec((ATT_HEADS, Q_TILE), lambda bi, qi: (0, 0)),
        ],
        out_specs=pl.BlockSpec((Q_TILE, width), lambda bi, qi: (bi * nq + qi, 0)),
        out_shape=jax.ShapeDtypeStruct((t, width), BF16),
        scratch_shapes=[
            pltpu.VMEM((s, LANES), BF16),
            pltpu.VMEM((s, Q_TILE), F32),
            pltpu.VMEM((1, Q_TILE), F32),
            pltpu.VMEM((1, Q_TILE), jnp.int32),
            pltpu.VMEM((IDX_HEADS, Q_TILE), F32),
            pltpu.VMEM((ATT_HEADS // 2, 2 * Q_TILE, LANES), BF16),
            pltpu.VMEM((IDX_HEADS // 2, 2 * Q_TILE, LANES), BF16),
            pltpu.VMEM((ATT_HEADS, Q_TILE), F32),
            pltpu.VMEM((ATT_HEADS, Q_TILE), F32),
            pltpu.VMEM((ATT_HEADS, ATT_HEAD_DIM, Q_TILE), F32),
            pltpu.VMEM((width, Q_TILE), F32),
            pltpu.VMEM((ATT_HEADS // 2, KEY_CHUNK, 2 * Q_TILE), F32),
            pltpu.VMEM((ATT_HEADS // 2, KEY_CHUNK, 2 * Q_TILE), BF16),
            pltpu.VMEM((ATT_HEADS, Q_TILE), F32),
        ],
        compiler_params=pltpu.CompilerParams(
            dimension_semantics=("arbitrary", "arbitrary"), vmem_limit_bytes=VMEM_LIMIT),
        name="dsa",
    )(u_big, u_big, u_small, u_small, k_all, vt_all, slopes)


def _merge_kernel(x_ref, ys_ref, ya_ref, g0_ref, g1_ref, gb_ref, wbs_ref, wba_ref, wo_ref,
                  nf_ref, wr_ref, br_ref, x1_ref, h2_ref, gate_ref, gselc_ref, gselr_ref):
    def sig(v):
        return 0.5 + 0.5 * jnp.tanh(0.5 * v)

    g0 = sig(g0_ref[...].astype(F32) + gb_ref[0:1, :])
    g1 = sig(g1_ref[...].astype(F32) + gb_ref[1:2, :])
    mixed = g0 * _dot(ys_ref[...], wbs_ref[...]) + g1 * _dot(ya_ref[...], wba_ref[...])
    x1 = x_ref[...] + _dot(mixed.astype(BF16), wo_ref[...])
    x1_ref[...] = x1
    h2 = x1 * lax.rsqrt(jnp.mean(x1 * x1, axis=-1, keepdims=True) + NORM_EPS) * nf_ref[...]
    h2b = h2.astype(BF16)
    h2_ref[...] = h2b

    n_e = MOE_GROUPS * MOE_EXPERTS
    logits = _dot(h2b, wr_ref[...]) + br_ref[...]
    lane = lax.broadcasted_iota(jnp.int32, logits.shape, 1)
    neg = -jnp.inf
    is_g = (lane >= n_e) & (lane < n_e + MOE_GROUPS)
    gl = jnp.where(is_g, logits, neg)
    gmax = jnp.max(gl, axis=-1, keepdims=True)
    g_val = 1.0 / jnp.sum(jnp.exp(gl - gmax), axis=-1, keepdims=True)
    g_sel = jnp.min(jnp.where(is_g & (gl == gmax), lane, 4 * LANES), axis=-1, keepdims=True) - n_e
    in_grp = (lane < n_e) & ((lane // MOE_EXPERTS) == g_sel)
    el = jnp.where(in_grp, logits, neg)
    m1 = jnp.max(el, axis=-1, keepdims=True)
    i1 = jnp.min(jnp.where(in_grp & (el == m1), lane, 4 * LANES), axis=-1, keepdims=True)
    el2 = jnp.where(lane == i1, neg, el)
    m2 = jnp.max(el2, axis=-1, keepdims=True)
    i2 = jnp.min(jnp.where(in_grp & (el2 == m2), lane, 4 * LANES), axis=-1, keepdims=True)
    p2 = jnp.exp(m2 - m1)
    inv = 1.0 / (1.0 + p2)
    gate = (jnp.where(lane == i1, inv, 0.0) + jnp.where(lane == i2, p2 * inv, 0.0)) * g_val
    gate8 = gate
    for g in range(1, MOE_GROUPS):
        gate8 = gate8 + pltpu.roll(gate, LANES - g * MOE_EXPERTS, 1)
    gate_ref[...] = jnp.where(lane < MOE_EXPERTS, gate8, 0.0)
    gsel = jnp.broadcast_to(g_sel.astype(F32), logits.shape)
    gselc_ref[...] = gsel
    gselr_ref[...] = jnp.transpose(gsel)[0:8, :]


def _merge(x2, y_ssm, y_att, u_big, gate_b, w_bs, w_ba, w_o, norm_ffn_g, w_r, b_r, tm):
    t, d = x2.shape
    d_in = y_ssm.shape[1]
    const = lambda shape: pl.BlockSpec(shape, lambda i: (0,) * len(shape))
    return pl.pallas_call(
        _merge_kernel,
        grid=(t // tm,),
        in_specs=[
            pl.BlockSpec((tm, d), lambda i: (i, 0)),
            pl.BlockSpec((tm, d_in), lambda i: (i, 0)),
            pl.BlockSpec((tm, d), lambda i: (i, 0)),
            pl.BlockSpec((tm, d), lambda i: (i, COL_G // d)),
            pl.BlockSpec((tm, d), lambda i: (i, COL_G // d + 1)),
            const((2, d)), const((d_in, d)), const((d, d)), const((d, d)), const((1, d)),
            const((d, LANES)), const((1, LANES)),
        ],
        out_specs=[
            pl.BlockSpec((tm, d), lambda i: (i, 0)),
            pl.BlockSpec((tm, d), lambda i: (i, 0)),
            pl.BlockSpec((tm, LANES), lambda i: (i, 0)),
            pl.BlockSpec((tm, LANES), lambda i: (i, 0)),
            pl.BlockSpec((8, tm), lambda i: (0, i)),
        ],
        out_shape=[
            jax.ShapeDtypeStruct((t, d), F32),
            jax.ShapeDtypeStruct((t, d), BF16),
            jax.ShapeDtypeStruct((t, LANES), F32),
            jax.ShapeDtypeStruct((t, LANES), F32),
            jax.ShapeDtypeStruct((8, t), F32),
        ],
        compiler_params=pltpu.CompilerParams(
            dimension_semantics=("arbitrary",), vmem_limit_bytes=VMEM_LIMIT),
        name="merge",
    )(x2, y_ssm, y_att, u_big, u_big, gate_b, w_bs, w_ba, w_o, norm_ffn_g, w_r, b_r)


MOE_SUB = 256
MOE_ROWS = 128
MOE_HALVES = 2


def _moe_kernel(h_ref, gate_ref, gselc_ref, gselr_ref, x1_ref, w1_ref, w3_ref, w2_ref, e_ref,
                triu_ref, tril_ref, nf_ref, o_ref, xc_ref, gc_ref, yc_ref, off_ref, cnt_ref):
    i, g, half = pl.program_id(0), pl.program_id(1), pl.program_id(2)
    tt = h_ref.shape[0]
    n_sub = tt // MOE_SUB
    g_f = g.astype(F32)

    @pl.when((i == 0) & (g == 0) & (half == 0))
    def _():
        xc_ref[...] = jnp.zeros(xc_ref.shape, BF16)
        gc_ref[...] = jnp.zeros(gc_ref.shape, F32)
        yc_ref[...] = jnp.zeros(yc_ref.shape, F32)

    @pl.when((g == 0) & (half == 0))
    def _():
        o_ref[...] = x1_ref[...]

    @pl.when(half == 0)
    def _():
        off = jnp.int32(0)
        for s in range(n_sub):
            tok = slice(s * MOE_SUB, (s + 1) * MOE_SUB)
            member = gselr_ref[:, tok] == g_f
            m_f = jnp.where(member, 1.0, 0.0)
            rank = _dot(m_f.astype(BF16), triu_ref[...])
            cnt = jnp.sum(m_f[0:1, :]).astype(jnp.int32)
            off_ref[s] = off
            cnt_ref[s] = cnt
            g8 = gate_ref[tok, :]
            g_hi = g8.astype(BF16)
            g_lo = (g8 - g_hi.astype(F32)).astype(BF16)

            rank = jnp.where(member, rank, -1.0)

            def compact(rbase, off=off, tok=tok, rank=rank, g_hi=g_hi, g_lo=g_lo):
                r = (rbase + lax.broadcasted_iota(jnp.int32, (MOE_ROWS, MOE_SUB), 0)).astype(F32)
                onehot = jnp.where(r == rank[0:1, :], 1.0, 0.0).astype(BF16)
                rows16 = pl.ds(pl.multiple_of(off + rbase, 16), MOE_ROWS)
                xc_ref[rows16, :] = _dot(onehot, h_ref[tok, :]).astype(BF16)
                gc_ref[rows16, :] = _dot(onehot, g_hi) + _dot(onehot, g_lo)

            compact(0)
            pl.when(cnt > MOE_ROWS)(functools.partial(compact, MOE_ROWS))
            off = off + ((cnt + 15) // 16) * 16
        off_ref[n_sub] = off

    def ffn_block(blk, carry):
        rows = pl.ds(pl.multiple_of(blk * MOE_ROWS, MOE_ROWS), MOE_ROWS)
        x = xc_ref[rows, :]
        gx = gc_ref[rows, :]
        gx_hi = gx.astype(BF16)
        gx_lo = (gx - gx_hi.astype(F32)).astype(BF16)
        gexp = _dot(gx_hi, e_ref[0]) + _dot(gx_lo, e_ref[0])
        eh, ff = w1_ref.shape[1], w1_ref.shape[-1]
        a = jnp.concatenate([_dot(x, w1_ref[0, e]) for e in range(eh)], axis=1)
        b3 = jnp.concatenate([_dot(x, w3_ref[0, e]) for e in range(eh)], axis=1)
        hid = (_silu(a) * b3 * gexp).astype(BF16)
        y = _dot(hid, w2_ref[0].reshape(eh * ff, w2_ref.shape[-1]))

        @pl.when(half == 0)
        def _():
            yc_ref[rows, :] = y

        @pl.when(half != 0)
        def _():
            yc_ref[rows, :] += y

        return carry

    lax.fori_loop(0, (off_ref[n_sub] + MOE_ROWS - 1) // MOE_ROWS, ffn_block, 0)

    @pl.when(half == MOE_HALVES - 1)
    def _():
        for s in range(n_sub):
            tok = slice(s * MOE_SUB, (s + 1) * MOE_SUB)
            member = gselc_ref[tok, :] == g_f
            rank = _dot(tril_ref[...], jnp.where(member, 1.0, 0.0).astype(BF16))
            rank = jnp.where(member, rank, -1.0)
            off = off_ref[s]

            def scatter(rbase, off=off, tok=tok, rank=rank):
                r = (rbase + lax.broadcasted_iota(jnp.int32, (MOE_SUB, MOE_ROWS), 1)).astype(F32)
                onehot_t = jnp.where(r == rank, 1.0, 0.0).astype(BF16)
                rows8 = pl.ds(pl.multiple_of(off + rbase, 8), MOE_ROWS)
                o_ref[tok, :] += _dot(onehot_t, yc_ref[rows8, :].astype(BF16))

            scatter(0)
            pl.when(cnt_ref[s] > MOE_ROWS)(functools.partial(scatter, MOE_ROWS))

    @pl.when((g == pl.num_programs(1) - 1) & (half == MOE_HALVES - 1))
    def _():
        x2 = o_ref[...]
        o_ref[...] = x2 * lax.rsqrt(jnp.mean(x2 * x2, axis=-1, keepdims=True) + NORM_EPS) * nf_ref[...]


def _moe(h2, gate8, gselc, gselr, x1, w1, w3, w2, norm_final_g, tt):
    t, d = x1.shape
    n_g, n_e, _, ff = w1.shape
    eh = n_e // MOE_HALVES
    assert MOE_ROWS == LANES and tt % MOE_SUB == 0 and n_e <= 8
    expand = np.zeros((MOE_HALVES, LANES, eh * ff), np.float32)
    for hf in range(MOE_HALVES):
        for e in range(eh):
            expand[hf, hf * eh + e, e * ff:(e + 1) * ff] = 1.0
    triu = np.triu(np.ones((MOE_SUB, MOE_SUB), np.float32), 1)
    xrows = tt + 2 * MOE_ROWS
    const = lambda shape: pl.BlockSpec(shape, lambda i, g, hf: (0,) * len(shape))
    tok = lambda width: pl.BlockSpec((tt, width), lambda i, g, hf: (i, 0))
    wspec = pl.BlockSpec((1, eh, d, ff), lambda i, g, hf: (g, hf, 0, 0))
    return pl.pallas_call(
        _moe_kernel,
        grid=(t // tt, n_g, MOE_HALVES),
        in_specs=[
            tok(d), tok(LANES), tok(LANES),
            pl.BlockSpec((8, tt), lambda i, g, hf: (0, i)),
            tok(d),
            wspec, wspec,
            pl.BlockSpec((1, eh, ff, d), lambda i, g, hf: (g, hf, 0, 0)),
            pl.BlockSpec((1, LANES, eh * ff), lambda i, g, hf: (hf, 0, 0)),
            const((MOE_SUB, MOE_SUB)), const((MOE_SUB, MOE_SUB)), const((1, d)),
        ],
        out_specs=pl.BlockSpec((tt, d), lambda i, g, hf: (i, 0)),
        out_shape=jax.ShapeDtypeStruct((t, d), F32),
        scratch_shapes=[
            pltpu.VMEM((xrows, d), BF16),
            pltpu.VMEM((xrows, LANES), F32),
            pltpu.VMEM((xrows, d), F32),
            pltpu.SMEM((8,), jnp.int32),
            pltpu.SMEM((8,), jnp.int32),
        ],
        compiler_params=pltpu.CompilerParams(
            dimension_semantics=("arbitrary", "arbitrary", "arbitrary"), vmem_limit_bytes=VMEM_LIMIT),
        name="moe",
    )(h2, gate8, gselc, gselr, x1, w1, w3, w2, jnp.asarray(expand, BF16),
      jnp.asarray(triu, BF16), jnp.asarray(triu.T, BF16), norm_final_g)


def _pick_tile(t, pref):
    tm = pref
    while t % tm:
        tm //= 2
    return tm


def kernel(x, norm_mix_g, w_in, conv_w, conv_b, dt_bias, a_log, d_skip, ssm_norm_g, kv_norm_g, w_uk, w_uv,
           gate_b, w_branch_ssm, w_branch_att, w_out, norm_ffn_g, w_group, b_group, w_router, b_router,
           w_e1, w_e3, w_e2, norm_final_g):
    b, s, d = x.shape
    depth = w_in.shape[0]
    t = b * s
    d_in = 2 * d
    heads = d_in // SSM_HEAD_DIM
    n_bc = 2 * SSM_GROUPS * SSM_STATE
    width = ATT_HEADS * ATT_HEAD_DIM
    qi_w = IDX_HEADS * IDX_HEAD_DIM
    assert (d, d_in, n_bc, width) == (1024, COL_XS - COL_Z, COL_Q - COL_BC, COL_QI - COL_Q)
    assert s % (2 * CHUNK) == 0 and s % Q_TILE == 0

    assert depth == 1, "the final RMSNorm is fused into the expert kernel of the only layer"
    layer = 0
    x2 = x.reshape(t, d)

    sizes = (d_in, d_in + n_bc, heads, width, KV_LATENT, qi_w, IDX_HEAD_DIM, IDX_HEADS, 2 * d)
    cuts = np.concatenate([[0], np.cumsum(sizes)])
    seg = lambda k: w_in[layer][:, int(cuts[k]):int(cuts[k + 1])]
    w_z, w_xbc, w_dt, w_q, w_ckv, w_qi, w_ki, w_wi, w_g = (seg(k) for k in range(9))
    zpad = lambda n: jnp.zeros((d, n), w_in.dtype)
    w_big = jnp.concatenate(
        [w_z, w_xbc, w_q, w_qi, w_ckv, zpad(COL_G - COL_CKV - KV_LATENT), w_g], axis=1).astype(BF16)
    w_small = jnp.concatenate(
        [w_ki, w_ki, w_dt, w_wi, zpad(N_SMALL - 2 * IDX_HEAD_DIM - heads - IDX_HEADS)], axis=1).astype(BF16)
    assert w_big.shape[1] == N_BIG and w_small.shape[1] == N_SMALL

    u_big, u_small = _in_proj(x2, norm_mix_g[layer][None, :], w_big, w_small, _pick_tile(t, 1024), 1536)

    y_ssm = _ssd(u_big, u_small, conv_w[layer], conv_b[layer], dt_bias[layer], a_log[layer],
                 d_skip[layer], ssm_norm_g[layer], b, s, 2)

    w_k = jnp.transpose(w_uk[layer], (1, 0, 2)).reshape(KV_LATENT, width).astype(BF16)
    w_vt = jnp.transpose(w_uv[layer], (0, 2, 1)).reshape(width, KV_LATENT).astype(BF16)
    k_all, vt_all = _kv_proj(u_big, kv_norm_g[layer][None, :], w_k, w_vt, b, s)
    y_att = _dsa(u_big, u_small, k_all, vt_all, b, s)

    n_e = MOE_GROUPS * MOE_EXPERTS
    w_r = jnp.concatenate(
        [jnp.transpose(w_router[layer], (1, 0, 2)).reshape(d, n_e), w_group[layer],
         jnp.zeros((d, LANES - n_e - MOE_GROUPS), F32)], axis=1).astype(BF16)
    b_r = jnp.concatenate(
        [b_router[layer].reshape(n_e), b_group[layer], jnp.zeros((LANES - n_e - MOE_GROUPS,), F32)])[None, :]
    x1, h2, gate8, gselc, gselr = _merge(
        x2, y_ssm, y_att, u_big, gate_b[layer],
        w_branch_ssm[layer].astype(BF16), w_branch_att[layer].astype(BF16),
        w_out[layer].astype(BF16), norm_ffn_g[layer][None, :], w_r, b_r, _pick_tile(t, 512))

    out = _moe(h2, gate8, gselc, gselr, x1, w_e1[layer].astype(BF16), w_e3[layer].astype(BF16),
               w_e2[layer].astype(BF16), norm_final_g[None, :], _pick_tile(t, 1024))
    return out.reshape(b, s, d)
```

```python
import functools

import numpy as np
import jax
import jax.numpy as jnp
from jax import lax
from jax.experimental import pallas as pl
from jax.experimental.pallas import tpu as pltpu

F32 = jnp.float32
BF16 = jnp.bfloat16

NORM_EPS = 1e-6
CHUNK = 64
SSM_HEAD_DIM = 64
SSM_GROUPS = 4
SSM_STATE = 128
SSM_CONV = 4
ATT_HEADS = 16
ATT_HEAD_DIM = 64
KV_LATENT = 256
IDX_HEADS = 8
IDX_HEAD_DIM = 64
TOPK_MAX = 256
MOE_GROUPS = 4
MOE_EXPERTS = 8
MOE_FF = 256

LANES = 128
Q_TILE = 256
CONV_PAD = 16
VMEM_LIMIT = 56 * 1024 * 1024
INT_MIN = -(2 ** 31)
LOG2E = 1.4426950408889634
ROW_SLAB = 64
KEY_CHUNK = 256
VT_ONES = 16
SOFTMAX_M0 = -1e30

COL_Z, COL_XS, COL_BC, COL_Q, COL_QI, COL_CKV, COL_G = 0, 2048, 4096, 5120, 6144, 6656, 7168
N_BIG = 9216
N_SMALL = 256
SM_DT, SM_W = 0, 32


def _nt_dot(a, b):
    return lax.dot_general(a, b, (((1,), (1,)), ((), ())), preferred_element_type=F32)


def _dot(a, b):
    return jnp.dot(a, b, preferred_element_type=F32)


def _split3(a):
    hi = a.astype(BF16)
    r = a - hi.astype(F32)
    mid = r.astype(BF16)
    lo = (r - mid.astype(F32)).astype(BF16)
    return hi, mid, lo


def _silu(v):
    hv = 0.5 * v
    return hv + hv * jnp.tanh(hv)


def _in_proj_kernel(x_ref, g_ref, wb_ref, ws_ref, ub_ref, us_ref, h_ref):
    j = pl.program_id(1)

    @pl.when(j == 0)
    def _():
        xf = x_ref[...]
        y = xf * lax.rsqrt(jnp.mean(xf * xf, axis=-1, keepdims=True) + NORM_EPS)
        h = (y * g_ref[...]).astype(BF16)
        h_ref[...] = h
        us_ref[...] = _dot(h, ws_ref[...])

    ub_ref[...] = _dot(h_ref[...], wb_ref[...]).astype(BF16)


def _in_proj(x2, g, w_big, w_small, tm, tn):
    t, d = x2.shape
    return pl.pallas_call(
        _in_proj_kernel,
        grid=(t // tm, N_BIG // tn),
        in_specs=[
            pl.BlockSpec((tm, d), lambda i, j: (i, 0)),
            pl.BlockSpec((1, d), lambda i, j: (0, 0)),
            pl.BlockSpec((d, tn), lambda i, j: (0, j)),
            pl.BlockSpec((d, N_SMALL), lambda i, j: (0, 0)),
        ],
        out_specs=[
            pl.BlockSpec((tm, tn), lambda i, j: (i, j)),
            pl.BlockSpec((tm, N_SMALL), lambda i, j: (i, 0)),
        ],
        out_shape=[
            jax.ShapeDtypeStruct((t, N_BIG), BF16),
            jax.ShapeDtypeStruct((t, N_SMALL), F32),
        ],
        scratch_shapes=[pltpu.VMEM((tm, d), BF16)],
        compiler_params=pltpu.CompilerParams(
            dimension_semantics=("arbitrary", "arbitrary"), vmem_limit_bytes=VMEM_LIMIT),
        name="in_proj",
    )(x2, g, w_big, w_small)


def _ssd_kernel(z_ref, xs_ref, bc_ref, sm_ref, cwx_ref, cwb_ref, cbx_ref, cbb_ref,
                dtb_ref, alog_ref, dsk_ref, ng_ref, e3_ref, tri_ref, it_ref, cm_ref, bm_ref, shift_ref,
                y_ref, xpx_ref, xpb_ref, st_ref, *, nch):
    lb = nch * CHUNK
    d_in = xs_ref.shape[-1]
    heads = d_in // SSM_HEAD_DIM
    gw = d_in // SSM_GROUPS
    gn = SSM_STATE

    @pl.when(pl.program_id(1) == 0)
    def _():
        xpx_ref[0:CONV_PAD, :] = jnp.zeros((CONV_PAD, d_in), BF16)
        xpb_ref[0:CONV_PAD, :] = jnp.zeros((CONV_PAD, xpb_ref.shape[-1]), BF16)
        st_ref[...] = jnp.zeros(st_ref.shape, F32)

    def conv(src_ref, pad_ref, w_ref, b_ref):
        cur = src_ref[...]
        pad_ref[CONV_PAD:CONV_PAD + lb, :] = cur
        delayed = _dot(shift_ref[...], pad_ref[...])
        acc = b_ref[...] + w_ref[SSM_CONV - 1:SSM_CONV, :] * cur.astype(F32)
        for k in range(SSM_CONV - 1):
            acc = acc + w_ref[k:k + 1, :] * delayed[k * lb:(k + 1) * lb, :]
        pad_ref[0:CONV_PAD, :] = pad_ref[lb:lb + CONV_PAD, :]
        return _silu(acc)

    xs = conv(xs_ref, xpx_ref, cwx_ref, cbx_ref)
    bc = conv(bc_ref, xpb_ref, cwb_ref, cbb_ref)

    dt_in = sm_ref[:, SM_DT:SM_DT + heads] + dtb_ref[...]
    dt = jnp.maximum(dt_in, 0.0) + jnp.log1p(jnp.exp(-jnp.abs(dt_in)))
    da = dt * (-jnp.exp(alog_ref[...]))
    tri = tri_ref[...]
    hi, mid, lo = _split3(da)
    a_cs = _dot(tri, hi) + _dot(tri, mid) + _dot(tri, lo)
    a_last = jnp.concatenate(
        [jnp.broadcast_to(a_cs[c * CHUNK + CHUNK - 1:c * CHUNK + CHUNK, :], (CHUNK, heads))
         for c in range(nch)], axis=0)
    stacked = jnp.concatenate([a_cs, jnp.exp(a_cs), jnp.exp(a_last - a_cs), dt], axis=0)
    s_hi, s_mid, s_lo = _split3(stacked)
    ex = _dot(jnp.concatenate([s_hi, s_mid, s_lo], axis=1), e3_ref[...])

    it_mask = it_ref[...] > 0.0
    cm_mask = cm_ref[...] > 0.0
    bm = bm_ref[...]
    it_bf = it_ref[...].astype(BF16)
    zf = z_ref[...].astype(F32)
    zg = _silu(zf)

    for c in range(nch):
        r0 = c * CHUNK
        a_l = ex[r0:r0 + CHUNK]
        exp_l = ex[lb + r0:lb + r0 + CHUNK]
        dst = ex[2 * lb + r0:2 * lb + r0 + CHUNK]
        dtx = ex[3 * lb + r0:3 * lb + r0 + CHUNK]
        xs_c = xs[r0:r0 + CHUNK]
        xdt = xs_c * dtx
        a_s = jnp.sum(jnp.where(it_mask, a_l, 0.0), axis=0, keepdims=True)
        decay = jnp.exp(jnp.where(cm_mask, a_l - a_s, -jnp.inf))
        xst = (xdt * dst).astype(BF16)
        xdt_bf = xdt.astype(BF16)
        y_parts = []
        for g in range(SSM_GROUPS):
            b_g = bc[r0:r0 + CHUNK, g * gn:(g + 1) * gn]
            c_g = bc[r0:r0 + CHUNK, (SSM_GROUPS + g) * gn:(SSM_GROUPS + g + 1) * gn]
            b_bf = b_g.astype(BF16)
            c_bf = c_g.astype(BF16)
            cb = _nt_dot(c_bf, b_bf)
            cbx = _dot(cb.astype(BF16), it_bf[:, 0:gw])
            w = (cbx * decay[:, g * gw:(g + 1) * gw]).astype(BF16)
            yd = []
            for jt in range(gw // 256):
                lo_c = g * gw + jt * 256
                xt = xdt_bf[:, lo_c:lo_c + 256]
                bd = jnp.concatenate([xt] * 4, axis=0) * bm
                yd.append(_dot(w[:, jt * 256:(jt + 1) * 256], bd))
            y_diag = jnp.concatenate(yd, axis=1)
            st_g = st_ref[g]
            y_off = _dot(c_bf, st_g.astype(BF16)) * exp_l[:, g * gw:(g + 1) * gw]
            new_st = _dot(jnp.transpose(b_g).astype(BF16), xst[:, g * gw:(g + 1) * gw])
            st_ref[g] = st_g * exp_l[CHUNK - 1:CHUNK, g * gw:(g + 1) * gw] + new_st
            y_g = y_diag + y_off + dsk_ref[:, g * gw:(g + 1) * gw] * xs_c[:, g * gw:(g + 1) * gw]
            y_g = y_g * zg[r0:r0 + CHUNK, g * gw:(g + 1) * gw]
            y_g = y_g * lax.rsqrt(jnp.mean(y_g * y_g, axis=-1, keepdims=True) + NORM_EPS)
            y_parts.append(y_g * ng_ref[:, g * gw:(g + 1) * gw])
        y_ref[r0:r0 + CHUNK, :] = jnp.concatenate(y_parts, axis=1).astype(BF16)


def _ssd(u_big, u_small, conv_w, conv_b, dt_bias, a_log, d_skip, ssm_norm_g, b, s, nch):
    t = b * s
    d_in = COL_BC - COL_XS
    n_bc = COL_Q - COL_BC
    heads = d_in // SSM_HEAD_DIM
    lb = nch * CHUNK
    nblk = s // lb
    assert d_in // SSM_GROUPS == 512 and SSM_HEAD_DIM == CHUNK

    e = np.kron(np.eye(heads, dtype=np.float32), np.ones((1, SSM_HEAD_DIM), np.float32))
    e3 = jnp.asarray(np.concatenate([e, e, e], axis=0), BF16)
    tri = jnp.asarray(np.kron(np.eye(nch, dtype=np.float32), np.tril(np.ones((CHUNK, CHUNK), np.float32))), BF16)
    it = jnp.asarray(np.tile(np.eye(CHUNK, dtype=np.float32), (1, heads)))
    cm = jnp.asarray(np.tile(np.tril(np.ones((CHUNK, CHUNK), np.float32)), (1, heads)))
    bm = jnp.asarray(np.kron(np.eye(4, dtype=np.float32), np.ones((64, 64), np.float32)), BF16)
    shift = np.zeros(((SSM_CONV - 1) * lb, CONV_PAD + lb), np.float32)
    for k in range(SSM_CONV - 1):
        shift[k * lb + np.arange(lb), CONV_PAD + np.arange(lb) - (SSM_CONV - 1 - k)] = 1.0
    shift = jnp.asarray(shift, BF16)

    cwx, cwb = conv_w[:, :d_in], conv_w[:, d_in:]
    cbx, cbb = conv_b[None, :d_in], conv_b[None, d_in:]
    dsk = jnp.repeat(d_skip, SSM_HEAD_DIM)[None, :]
    const = lambda shape: pl.BlockSpec(shape, lambda bi, ci: (0,) * len(shape))
    row = lambda width, col: pl.BlockSpec((lb, width), lambda bi, ci: (bi * nblk + ci, col))

    return pl.pallas_call(
        functools.partial(_ssd_kernel, nch=nch),
        grid=(b, nblk),
        in_specs=[
            row(d_in, COL_Z // d_in), row(d_in, COL_XS // d_in), row(n_bc, COL_BC // n_bc),
            row(LANES, 1),
            const((SSM_CONV, d_in)), const((SSM_CONV, n_bc)), const((1, d_in)), const((1, n_bc)),
            const((1, heads)), const((1, heads)), const((1, d_in)), const((1, d_in)),
            const((3 * heads, d_in)), const((lb, lb)), const((CHUNK, d_in)), const((CHUNK, d_in)),
            const((256, 256)), const(((SSM_CONV - 1) * lb, CONV_PAD + lb)),
        ],
        out_specs=pl.BlockSpec((lb, d_in), lambda bi, ci: (bi * nblk + ci, 0)),
        out_shape=jax.ShapeDtypeStruct((t, d_in), BF16),
        scratch_shapes=[
            pltpu.VMEM((lb + CONV_PAD, d_in), BF16),
            pltpu.VMEM((lb + CONV_PAD, n_bc), BF16),
            pltpu.VMEM((SSM_GROUPS, SSM_STATE, d_in // SSM_GROUPS), F32),
        ],
        compiler_params=pltpu.CompilerParams(
            dimension_semantics=("arbitrary", "arbitrary"), vmem_limit_bytes=VMEM_LIMIT),
        name="ssd",
    )(u_big, u_big, u_big, u_small, cwx, cwb, cbx, cbb, dt_bias[None, :], a_log[None, :], dsk,
      ssm_norm_g[None, :], e3, tri, it, cm, bm, shift)


def _kv_proj_kernel(c_ref, g_ref, wk_ref, wvt_ref, k_ref, vt_ref):
    cf = c_ref[...].astype(F32)
    y = cf * lax.rsqrt(jnp.mean(cf * cf, axis=-1, keepdims=True) + NORM_EPS)
    cn = (y * g_ref[...]).astype(BF16)
    k_ref[0] = (_dot(cn, wk_ref[...]) * LOG2E).astype(BF16)
    vt = _nt_dot(wvt_ref[...], cn).astype(BF16)
    for pair in range(ATT_HEADS // 2):
        vt_ref[0, pair, 0:LANES, :] = vt[pair * LANES:(pair + 1) * LANES, :]
        vt_ref[0, pair, LANES:, :] = jnp.ones((VT_ONES, vt.shape[1]), BF16)


def _kv_proj(u_big, kv_norm_g, w_k, w_vt, b, s):
    width = ATT_HEADS * ATT_HEAD_DIM
    return pl.pallas_call(
        _kv_proj_kernel,
        grid=(b,),
        in_specs=[
            pl.BlockSpec((s, KV_LATENT), lambda bi: (bi, COL_CKV // KV_LATENT)),
            pl.BlockSpec((1, KV_LATENT), lambda bi: (0, 0)),
            pl.BlockSpec((KV_LATENT, width), lambda bi: (0, 0)),
            pl.BlockSpec((width, KV_LATENT), lambda bi: (0, 0)),
        ],
        out_specs=[
            pl.BlockSpec((1, s, width), lambda bi: (bi, 0, 0)),
            pl.BlockSpec((1, ATT_HEADS // 2, LANES + VT_ONES, s), lambda bi: (bi, 0, 0, 0)),
        ],
        out_shape=[
            jax.ShapeDtypeStruct((b, s, width), BF16),
            jax.ShapeDtypeStruct((b, ATT_HEADS // 2, LANES + VT_ONES, s), BF16),
        ],
        compiler_params=pltpu.CompilerParams(
            dimension_semantics=("arbitrary",), vmem_limit_bytes=VMEM_LIMIT),
        name="kv_proj",
    )(u_big, kv_norm_g, w_k, w_vt)


def _col_reduce(x, op):
    n = x.shape[0]
    slab = op(x.reshape(n // ROW_SLAB, ROW_SLAB, x.shape[1]), axis=0)
    return op(slab, axis=0, keepdims=True)


def _dsa_kernel(q_ref, qi_ref, smq_ref, smk_ref, k_ref, vt_ref, slope_ref, o_ref,
                kidx_ref, sc_ref, thr_ref, jcut_ref, wt_ref, q2_ref, qi2_ref, m_ref, l_ref, acc_ref, ot_ref,
                att_ref, p_ref, alpha_ref, *, n_buckets, topk):
    i = pl.program_id(1)

    @pl.when(i == 0)
    def _():
        kidx_ref[...] = smk_ref[...].astype(BF16)

    lane = lax.broadcasted_iota(jnp.int32, (Q_TILE, LANES), 1)
    low_half = lane < IDX_HEAD_DIM

    def head_pair_rows(x):
        zero = jnp.zeros_like(x)
        return jnp.concatenate([jnp.where(low_half, x, zero), jnp.where(low_half, zero, x)], axis=0)

    wt_ref[...] = jnp.transpose(smq_ref[...])[SM_W:SM_W + IDX_HEADS, :] * (IDX_HEADS ** -0.5 * IDX_HEAD_DIM ** -0.5)
    for pair in range(IDX_HEADS // 2):
        qi2_ref[pair] = head_pair_rows(qi_ref[:, pair * LANES:(pair + 1) * LANES])
    for pair in range(ATT_HEADS // 2):
        q2_ref[pair] = head_pair_rows(
            q_ref[:, pair * LANES:(pair + 1) * LANES] * jnp.asarray(ATT_HEAD_DIM ** -0.5, BF16))
    kf = float(topk)

    n_chunks = ((i + 1) * Q_TILE + KEY_CHUNK - 1) // KEY_CHUNK
    row_iota = lax.broadcasted_iota(jnp.int32, (KEY_CHUNK, Q_TILE), 0)
    t_pos = i * Q_TILE + lax.broadcasted_iota(jnp.int32, (KEY_CHUNK, Q_TILE), 1)
    vis_end = (t_pos // CHUNK + 1) * CHUNK

    def chunk_rows(c):
        return pl.ds(pl.multiple_of(c * KEY_CHUNK, KEY_CHUNK), KEY_CHUNK)

    def index_chunk(c, carry):
        rows = chunk_rows(c)
        score = jnp.zeros((KEY_CHUNK, Q_TILE), F32)
        for pair in range(IDX_HEADS // 2):
            logit = _nt_dot(kidx_ref[rows, :], qi2_ref[pair])
            score = (score + jnp.maximum(logit[:, 0:Q_TILE], 0.0) * wt_ref[2 * pair:2 * pair + 1, :]
                     + jnp.maximum(logit[:, Q_TILE:], 0.0) * wt_ref[2 * pair + 1:2 * pair + 2, :])
        sc_ref[rows, :] = jnp.where(c * KEY_CHUNK + row_iota < vis_end, score, -jnp.inf)
        return carry

    lax.fori_loop(0, n_chunks, index_chunk, 0)

    many = vis_end[0:1, :] > topk

    def select(n):
        s_pos = lax.broadcasted_iota(jnp.int32, (n, Q_TILE), 0)

        def count(pred):
            return _col_reduce(jnp.where(pred, 1.0, 0.0), jnp.sum)

        def bit_step(it, u):
            cand = u | (jnp.int32(1) << (31 - it))
            cand_f = pltpu.bitcast(jnp.where(cand < 0, cand ^ INT_MIN, ~cand), F32)
            c = count(sc_ref[0:n, :] >= cand_f)
            return jnp.where(c >= kf, cand, u)

        u_thr = lax.fori_loop(0, 32, bit_step, jnp.zeros((1, Q_TILE), jnp.int32))
        thr = pltpu.bitcast(jnp.where(u_thr < 0, u_thr ^ INT_MIN, ~u_thr), F32)
        thr = jnp.where(many, thr, -jnp.inf)
        thr_ref[...] = thr
        sc = sc_ref[0:n, :]
        gt = sc > thr
        eq = sc == thr
        need = kf - count(gt)
        excess = jnp.where(many, count(eq) - need, 0.0)
        jcut_ref[...] = jnp.where(many, n, -1)

        @pl.when(jnp.max(excess) > 0.0)
        def _():
            nbits = max(1, int(np.ceil(np.log2(n))))

            def idx_step(it, j0):
                cand = j0 | (jnp.int32(1) << (nbits - 1 - it))
                c = count((sc_ref[0:n, :] == thr) & (s_pos < cand))
                return jnp.where(c < need, cand, j0)

            j0 = lax.fori_loop(0, nbits, idx_step, jnp.zeros((1, Q_TILE), jnp.int32))
            jcut_ref[...] = jnp.where(many, j0, -1)

    for nb in range(1, n_buckets + 1):
        pl.when(n_chunks == nb)(functools.partial(select, nb * KEY_CHUNK))

    m_ref[...] = jnp.full(m_ref.shape, SOFTMAX_M0, F32)
    l_ref[...] = jnp.zeros(l_ref.shape, F32)
    acc_ref[...] = jnp.zeros(acc_ref.shape, F32)
    hd = ATT_HEAD_DIM

    def attend_chunk(c, carry):
        rows = chunk_rows(c)
        s_pos = c * KEY_CHUNK + row_iota
        sc = sc_ref[rows, :]
        sel = (sc > thr_ref[...]) | ((sc == thr_ref[...]) & (s_pos <= jcut_ref[...]))
        dsel = jnp.where(sel, jnp.abs(t_pos - s_pos).astype(F32), jnp.inf)
        for pair in range(ATT_HEADS // 2):
            att_ref[pair] = _nt_dot(k_ref[0, rows, pair * LANES:(pair + 1) * LANES], q2_ref[pair])
        for h in range(ATT_HEADS):
            pair, half = divmod(h, 2)
            cols = slice(half * Q_TILE, (half + 1) * Q_TILE)
            a = att_ref[pair, :, cols] - slope_ref[h:h + 1, :] * dsel
            m_old = m_ref[h:h + 1, :]
            m_new = jnp.maximum(m_old, _col_reduce(a, jnp.max))
            m_ref[h:h + 1, :] = m_new
            alpha_ref[h:h + 1, :] = jnp.exp2(m_old - m_new)
            p_ref[pair, :, cols] = jnp.exp2(a - m_new).astype(BF16)
        for pair in range(ATT_HEADS // 2):
            o2 = _dot(vt_ref[0, pair, :, rows], p_ref[pair])
            for half in range(2):
                h = 2 * pair + half
                cols = slice(half * Q_TILE, (half + 1) * Q_TILE)
                alpha = alpha_ref[h:h + 1, :]
                acc_ref[h] = acc_ref[h] * alpha + o2[half * hd:(half + 1) * hd, cols]
                l_ref[h:h + 1, :] = l_ref[h:h + 1, :] * alpha + o2[2 * hd:2 * hd + 1, cols]
        return carry

    lax.fori_loop(0, n_chunks, attend_chunk, 0)
    for h in range(ATT_HEADS):
        ot_ref[h * hd:(h + 1) * hd, :] = acc_ref[h] * (1.0 / l_ref[h:h + 1, :])
    o_ref[...] = jnp.transpose(ot_ref[...]).astype(BF16)


def _dsa(u_big, u_small, k_all, vt_all, b, s):
    t = b * s
    width = ATT_HEADS * ATT_HEAD_DIM
    nq = s // Q_TILE
    topk = min(TOPK_MAX, s // 4)
    assert s % KEY_CHUNK == 0
    slopes = 2.0 ** (-8.0 * jnp.arange(1, ATT_HEADS + 1, dtype=F32) / ATT_HEADS) * LOG2E
    slopes = jnp.broadcast_to(slopes[:, None], (ATT_HEADS, Q_TILE))
    qi_w = IDX_HEADS * IDX_HEAD_DIM
    return pl.pallas_call(
        functools.partial(_dsa_kernel, n_buckets=s // KEY_CHUNK, topk=topk),
        grid=(b, nq),
        in_specs=[
            pl.BlockSpec((Q_TILE, width), lambda bi, qi: (bi * nq + qi, COL_Q // width)),
            pl.BlockSpec((Q_TILE, qi_w), lambda bi, qi: (bi * nq + qi, COL_QI // qi_w)),
            pl.BlockSpec((Q_TILE, LANES), lambda bi, qi: (bi * nq + qi, 1)),
            pl.BlockSpec((s, LANES), lambda bi, qi: (bi, 0)),
            pl.BlockSpec((1, s, width), lambda bi, qi: (bi, 0, 0)),
            pl.BlockSpec((1, ATT_HEADS // 2, LANES + VT_ONES, s), lambda bi, qi: (bi, 0, 0, 0)),
            pl.BlockSpec((ATT_HEADS, Q_TILE), lambda bi, qi: (0, 0)),
        ],
        out_specs=pl.BlockSpec((Q_TILE, width), lambda bi, qi: (bi * nq + qi, 0)),
        out_shape=jax.ShapeDtypeStruct((t, width), BF16),
        scratch_shapes=[
            pltpu.VMEM((s, LANES), BF16),
            pltpu.VMEM((s, Q_TILE), F32),
            pltpu.VMEM((1, Q_TILE), F32),
            pltpu.VMEM((1, Q_TILE), jnp.int32),
            pltpu.VMEM((IDX_HEADS, Q_TILE), F32),
            pltpu.VMEM((ATT_HEADS // 2, 2 * Q_TILE, LANES), BF16),
            pltpu.VMEM((IDX_HEADS // 2, 2 * Q_TILE, LANES), BF16),
            pltpu.VMEM((ATT_HEADS, Q_TILE), F32),
            pltpu.VMEM((ATT_HEADS, Q_TILE), F32),
            pltpu.VMEM((ATT_HEADS, ATT_HEAD_DIM, Q_TILE), F32),
            pltpu.VMEM((width, Q_TILE), F32),
            pltpu.VMEM((ATT_HEADS // 2, KEY_CHUNK, 2 * Q_TILE), F32),
            pltpu.VMEM((ATT_HEADS // 2, KEY_CHUNK, 2 * Q_TILE), BF16),
            pltpu.VMEM((ATT_HEADS, Q_TILE), F32),
        ],
        compiler_params=pltpu.CompilerParams(
            dimension_semantics=("arbitrary", "arbitrary"), vmem_limit_bytes=VMEM_LIMIT),
        name="dsa",
    )(u_big, u_big, u_small, u_small, k_all, vt_all, slopes)


def _merge_kernel(x_ref, ys_ref, ya_ref, g0_ref, g1_ref, gb_ref, wbs_ref, wba_ref, wo_ref,
                  nf_ref, wr_ref, br_ref, x1_ref, h2_ref, gate_ref, gselc_ref, gselr_ref):
    def sig(v):
        return 0.5 + 0.5 * jnp.tanh(0.5 * v)

    g0 = sig(g0_ref[...].astype(F32) + gb_ref[0:1, :])
    g1 = sig(g1_ref[...].astype(F32) + gb_ref[1:2, :])
    mixed = g0 * _dot(ys_ref[...], wbs_ref[...]) + g1 * _dot(ya_ref[...], wba_ref[...])
    x1 = x_ref[...] + _dot(mixed.astype(BF16), wo_ref[...])
    x1_ref[...] = x1
    h2 = x1 * lax.rsqrt(jnp.mean(x1 * x1, axis=-1, keepdims=True) + NORM_EPS) * nf_ref[...]
    h2b = h2.astype(BF16)
    h2_ref[...] = h2b

    n_e = MOE_GROUPS * MOE_EXPERTS
    logits = _dot(h2b, wr_ref[...]) + br_ref[...]
    lane = lax.broadcasted_iota(jnp.int32, logits.shape, 1)
    neg = -jnp.inf
    is_g = (lane >= n_e) & (lane < n_e + MOE_GROUPS)
    gl = jnp.where(is_g, logits, neg)
    gmax = jnp.max(gl, axis=-1, keepdims=True)
    g_val = 1.0 / jnp.sum(jnp.exp(gl - gmax), axis=-1, keepdims=True)
    g_sel = jnp.min(jnp.where(is_g & (gl == gmax), lane, 4 * LANES), axis=-1, keepdims=True) - n_e
    in_grp = (lane < n_e) & ((lane // MOE_EXPERTS) == g_sel)
    el = jnp.where(in_grp, logits, neg)
    m1 = jnp.max(el, axis=-1, keepdims=True)
    i1 = jnp.min(jnp.where(in_grp & (el == m1), lane, 4 * LANES), axis=-1, keepdims=True)
    el2 = jnp.where(lane == i1, neg, el)
    m2 = jnp.max(el2, axis=-1, keepdims=True)
    i2 = jnp.min(jnp.where(in_grp & (el2 == m2), lane, 4 * LANES), axis=-1, keepdims=True)
    p2 = jnp.exp(m2 - m1)
    inv = 1.0 / (1.0 + p2)
    gate = (jnp.where(lane == i1, inv, 0.0) + jnp.where(lane == i2, p2 * inv, 0.0)) * g_val
    gate8 = gate
    for g in range(1, MOE_GROUPS):
        gate8 = gate8 + pltpu.roll(gate, LANES - g * MOE_EXPERTS, 1)
    gate_ref[...] = jnp.where(lane < MOE_EXPERTS, gate8, 0.0)
    gsel = jnp.broadcast_to(g_sel.astype(F32), logits.shape)
    gselc_ref[...] = gsel
    gselr_ref[...] = jnp.transpose(gsel)[0:8, :]


def _merge(x2, y_ssm, y_att, u_big, gate_b, w_bs, w_ba, w_o, norm_ffn_g, w_r, b_r, tm):
    t, d = x2.shape
    d_in = y_ssm.shape[1]
    const = lambda shape: pl.BlockSpec(shape, lambda i: (0,) * len(shape))
    return pl.pallas_call(
        _merge_kernel,
        grid=(t // tm,),
        in_specs=[
            pl.BlockSpec((tm, d), lambda i: (i, 0)),
            pl.BlockSpec((tm, d_in), lambda i: (i, 0)),
            pl.BlockSpec((tm, d), lambda i: (i, 0)),
            pl.BlockSpec((tm, d), lambda i: (i, COL_G // d)),
            pl.BlockSpec((tm, d), lambda i: (i, COL_G // d + 1)),
            const((2, d)), const((d_in, d)), const((d, d)), const((d, d)), const((1, d)),
            const((d, LANES)), const((1, LANES)),
        ],
        out_specs=[
            pl.BlockSpec((tm, d), lambda i: (i, 0)),
            pl.BlockSpec((tm, d), lambda i: (i, 0)),
            pl.BlockSpec((tm, LANES), lambda i: (i, 0)),
            pl.BlockSpec((tm, LANES), lambda i: (i, 0)),
            pl.BlockSpec((8, tm), lambda i: (0, i)),
        ],
        out_shape=[
            jax.ShapeDtypeStruct((t, d), F32),
            jax.ShapeDtypeStruct((t, d), BF16),
            jax.ShapeDtypeStruct((t, LANES), F32),
            jax.ShapeDtypeStruct((t, LANES), F32),
            jax.ShapeDtypeStruct((8, t), F32),
        ],
        compiler_params=pltpu.CompilerParams(
            dimension_semantics=("arbitrary",), vmem_limit_bytes=VMEM_LIMIT),
        name="merge",
    )(x2, y_ssm, y_att, u_big, u_big, gate_b, w_bs, w_ba, w_o, norm_ffn_g, w_r, b_r)


MOE_SUB = 256
MOE_ROWS = 128
MOE_HALVES = 2


def _moe_kernel(h_ref, gate_ref, gselc_ref, gselr_ref, x1_ref, w1_ref, w3_ref, w2_ref, e_ref,
                triu_ref, tril_ref, nf_ref, o_ref, xc_ref, gc_ref, yc_ref, off_ref, cnt_ref):
    i, g, half = pl.program_id(0), pl.program_id(1), pl.program_id(2)
    tt = h_ref.shape[0]
    n_sub = tt // MOE_SUB
    g_f = g.astype(F32)

    @pl.when((i == 0) & (g == 0) & (half == 0))
    def _():
        xc_ref[...] = jnp.zeros(xc_ref.shape, BF16)
        gc_ref[...] = jnp.zeros(gc_ref.shape, F32)
        yc_ref[...] = jnp.zeros(yc_ref.shape, F32)

    @pl.when((g == 0) & (half == 0))
    def _():
        o_ref[...] = x1_ref[...]

    @pl.when(half == 0)
    def _():
        off = jnp.int32(0)
        for s in range(n_sub):
            tok = slice(s * MOE_SUB, (s + 1) * MOE_SUB)
            member = gselr_ref[:, tok] == g_f
            m_f = jnp.where(member, 1.0, 0.0)
            rank = _dot(m_f.astype(BF16), triu_ref[...])
            cnt = jnp.sum(m_f[0:1, :]).astype(jnp.int32)
            off_ref[s] = off
            cnt_ref[s] = cnt
            g8 = gate_ref[tok, :]
            g_hi = g8.astype(BF16)
            g_lo = (g8 - g_hi.astype(F32)).astype(BF16)

            rank = jnp.where(member, rank, -1.0)

            def compact(rbase, off=off, tok=tok, rank=rank, g_hi=g_hi, g_lo=g_lo):
                r = (rbase + lax.broadcasted_iota(jnp.int32, (MOE_ROWS, MOE_SUB), 0)).astype(F32)
                onehot = jnp.where(r == rank[0:1, :], 1.0, 0.0).astype(BF16)
                rows16 = pl.ds(pl.multiple_of(off + rbase, 16), MOE_ROWS)
                xc_ref[rows16, :] = _dot(onehot, h_ref[tok, :]).astype(BF16)
                gc_ref[rows16, :] = _dot(onehot, g_hi) + _dot(onehot, g_lo)

            compact(0)
            pl.when(cnt > MOE_ROWS)(functools.partial(compact, MOE_ROWS))
            off = off + ((cnt + 15) // 16) * 16
        off_ref[n_sub] = off

    def ffn_block(blk, carry):
        rows = pl.ds(pl.multiple_of(blk * MOE_ROWS, MOE_ROWS), MOE_ROWS)
        x = xc_ref[rows, :]
        gx = gc_ref[rows, :]
        gx_hi = gx.astype(BF16)
        gx_lo = (gx - gx_hi.astype(F32)).astype(BF16)
        gexp = _dot(gx_hi, e_ref[0]) + _dot(gx_lo, e_ref[0])
        eh, ff = w1_ref.shape[1], w1_ref.shape[-1]
        a = jnp.concatenate([_dot(x, w1_ref[0, e]) for e in range(eh)], axis=1)
        b3 = jnp.concatenate([_dot(x, w3_ref[0, e]) for e in range(eh)], axis=1)
        hid = (_silu(a) * b3 * gexp).astype(BF16)
        y = _dot(hid, w2_ref[0].reshape(eh * ff, w2_ref.shape[-1]))

        @pl.when(half == 0)
        def _():
            yc_ref[rows, :] = y

        @pl.when(half != 0)
        def _():
            yc_ref[rows, :] += y

        return carry

    lax.fori_loop(0, (off_ref[n_sub] + MOE_ROWS - 1) // MOE_ROWS, ffn_block, 0)

    @pl.when(half == MOE_HALVES - 1)
    def _():
        for s in range(n_sub):
            tok = slice(s * MOE_SUB, (s + 1) * MOE_SUB)
            member = gselc_ref[tok, :] == g_f
            rank = _dot(tril_ref[...], jnp.where(member, 1.0, 0.0).astype(BF16))
            rank = jnp.where(member, rank, -1.0)
            off = off_ref[s]

            def scatter(rbase, off=off, tok=tok, rank=rank):
                r = (rbase + lax.broadcasted_iota(jnp.int32, (MOE_SUB, MOE_ROWS), 1)).astype(F32)
                onehot_t = jnp.where(r == rank, 1.0, 0.0).astype(BF16)
                rows8 = pl.ds(pl.multiple_of(off + rbase, 8), MOE_ROWS)
                o_ref[tok, :] += _dot(onehot_t, yc_ref[rows8, :].astype(BF16))

            scatter(0)
            pl.when(cnt_ref[s] > MOE_ROWS)(functools.partial(scatter, MOE_ROWS))

    @pl.when((g == pl.num_programs(1) - 1) & (half == MOE_HALVES - 1))
    def _():
        x2 = o_ref[...]
        o_ref[...] = x2 * lax.rsqrt(jnp.mean(x2 * x2, axis=-1, keepdims=True) + NORM_EPS) * nf_ref[...]


def _moe(h2, gate8, gselc, gselr, x1, w1, w3, w2, norm_final_g, tt):
    t, d = x1.shape
    n_g, n_e, _, ff = w1.shape
    eh = n_e // MOE_HALVES
    assert MOE_ROWS == LANES and tt % MOE_SUB == 0 and n_e <= 8
    expand = np.zeros((MOE_HALVES, LANES, eh * ff), np.float32)
    for hf in range(MOE_HALVES):
        for e in range(eh):
            expand[hf, hf * eh + e, e * ff:(e + 1) * ff] = 1.0
    triu = np.triu(np.ones((MOE_SUB, MOE_SUB), np.float32), 1)
    xrows = tt + 2 * MOE_ROWS
    const = lambda shape: pl.BlockSpec(shape, lambda i, g, hf: (0,) * len(shape))
    tok = lambda width: pl.BlockSpec((tt, width), lambda i, g, hf: (i, 0))
    wspec = pl.BlockSpec((1, eh, d, ff), lambda i, g, hf: (g, hf, 0, 0))
    return pl.pallas_call(
        _moe_kernel,
        grid=(t // tt, n_g, MOE_HALVES),
        in_specs=[
            tok(d), tok(LANES), tok(LANES),
            pl.BlockSpec((8, tt), lambda i, g, hf: (0, i)),
            tok(d),
            wspec, wspec,
            pl.BlockSpec((1, eh, ff, d), lambda i, g, hf: (g, hf, 0, 0)),
            pl.BlockSpec((1, LANES, eh * ff), lambda i, g, hf: (hf, 0, 0)),
            const((MOE_SUB, MOE_SUB)), const((MOE_SUB, MOE_SUB)), const((1, d)),
        ],
        out_specs=pl.BlockSpec((tt, d), lambda i, g, hf: (i, 0)),
        out_shape=jax.ShapeDtypeStruct((t, d), F32),
        scratch_shapes=[
            pltpu.VMEM((xrows, d), BF16),
            pltpu.VMEM((xrows, LANES), F32),
            pltpu.VMEM((xrows, d), F32),
            pltpu.SMEM((8,), jnp.int32),
            pltpu.SMEM((8,), jnp.int32),
        ],
        compiler_params=pltpu.CompilerParams(
            dimension_semantics=("arbitrary", "arbitrary", "arbitrary"), vmem_limit_bytes=VMEM_LIMIT),
        name="moe",
    )(h2, gate8, gselc, gselr, x1, w1, w3, w2, jnp.asarray(expand, BF16),
      jnp.asarray(triu, BF16), jnp.asarray(triu.T, BF16), norm_final_g)


def _pick_tile(t, pref):
    tm = pref
    while t % tm:
        tm //= 2
    return tm


def kernel(x, norm_mix_g, w_in, conv_w, conv_b, dt_bias, a_log, d_skip, ssm_norm_g, kv_norm_g, w_uk, w_uv,
           gate_b, w_branch_ssm, w_branch_att, w_out, norm_ffn_g, w_group, b_group, w_router, b_router,
           w_e1, w_e3, w_e2, norm_final_g):
    b, s, d = x.shape
    depth = w_in.shape[0]
    t = b * s
    d_in = 2 * d
    heads = d_in // SSM_HEAD_DIM
    n_bc = 2 * SSM_GROUPS * SSM_STATE
    width = ATT_HEADS * ATT_HEAD_DIM
    qi_w = IDX_HEADS * IDX_HEAD_DIM
    assert (d, d_in, n_bc, width) == (1024, COL_XS - COL_Z, COL_Q - COL_BC, COL_QI - COL_Q)
    assert s % (2 * CHUNK) == 0 and s % Q_TILE == 0

    assert depth == 1, "the final RMSNorm is fused into the expert kernel of the only layer"
    layer = 0
    x2 = x.reshape(t, d)

    sizes = (d_in, d_in + n_bc, heads, width, KV_LATENT, qi_w, IDX_HEAD_DIM, IDX_HEADS, 2 * d)
    cuts = np.concatenate([[0], np.cumsum(sizes)])
    seg = lambda k: w_in[layer][:, int(cuts[k]):int(cuts[k + 1])]
    w_z, w_xbc, w_dt, w_q, w_ckv, w_qi, w_ki, w_wi, w_g = (seg(k) for k in range(9))
    zpad = lambda n: jnp.zeros((d, n), w_in.dtype)
    w_big = jnp.concatenate(
        [w_z, w_xbc, w_q, w_qi, w_ckv, zpad(COL_G - COL_CKV - KV_LATENT), w_g], axis=1).astype(BF16)
    w_small = jnp.concatenate(
        [w_ki, w_ki, w_dt, w_wi, zpad(N_SMALL - 2 * IDX_HEAD_DIM - heads - IDX_HEADS)], axis=1).astype(BF16)
    assert w_big.shape[1] == N_BIG and w_small.shape[1] == N_SMALL

    u_big, u_small = _in_proj(x2, norm_mix_g[layer][None, :], w_big, w_small, _pick_tile(t, 1024), 1536)

    y_ssm = _ssd(u_big, u_small, conv_w[layer], conv_b[layer], dt_bias[layer], a_log[layer],
                 d_skip[layer], ssm_norm_g[layer], b, s, 2)

    w_k = jnp.transpose(w_uk[layer], (1, 0, 2)).reshape(KV_LATENT, width).astype(BF16)
    w_vt = jnp.transpose(w_uv[layer], (0, 2, 1)).reshape(width, KV_LATENT).astype(BF16)
    k_all, vt_all = _kv_proj(u_big, kv_norm_g[layer][None, :], w_k, w_vt, b, s)
    y_att = _dsa(u_big, u_small, k_all, vt_all, b, s)

    n_e = MOE_GROUPS * MOE_EXPERTS
    w_r = jnp.concatenate(
        [jnp.transpose(w_router[layer], (1, 0, 2)).reshape(d, n_e), w_group[layer],
         jnp.zeros((d, LANES - n_e - MOE_GROUPS), F32)], axis=1).astype(BF16)
    b_r = jnp.concatenate(
        [b_router[layer].reshape(n_e), b_group[layer], jnp.zeros((LANES - n_e - MOE_GROUPS,), F32)])[None, :]
    x1, h2, gate8, gselc, gselr = _merge(
        x2, y_ssm, y_att, u_big, gate_b[layer],
        w_branch_ssm[layer].astype(BF16), w_branch_att[layer].astype(BF16),
        w_out[layer].astype(BF16), norm_ffn_g[layer][None, :], w_r, b_r, _pick_tile(t, 512))

    out = _moe(h2, gate8, gselc, gselr, x1, w_e1[layer].astype(BF16), w_e3[layer].astype(BF16),
               w_e2[layer].astype(BF16), norm_final_g[None, :], _pick_tile(t, 1024))
    return out.reshape(b, s, d)
```

```python
import functools

import numpy as np
import jax
import jax.numpy as jnp
from jax import lax
from jax.experimental import pallas as pl
from jax.experimental.pallas import tpu as pltpu

F32 = jnp.float32
BF16 = jnp.bfloat16

NORM_EPS = 1e-6
CHUNK = 64
SSM_HEAD_DIM = 64
SSM_GROUPS = 4
SSM_STATE = 128
SSM_CONV = 4
ATT_HEADS = 16
ATT_HEAD_DIM = 64
KV_LATENT = 256
IDX_HEADS = 8
IDX_HEAD_DIM = 64
TOPK_MAX = 256
MOE_GROUPS = 4
MOE_EXPERTS = 8
MOE_FF = 256

LANES = 128
Q_TILE = 256
CONV_PAD = 16
VMEM_LIMIT = 56 * 1024 * 1024
INT_MIN = -(2 ** 31)
LOG2E = 1.4426950408889634
ROW_SLAB = 64
KEY_CHUNK = 256
BIT_SEARCH_FIRST = 16
BIT_SEARCH_STEP = 4
VT_ONES = 16
SOFTMAX_M0 = -1e30

COL_Z, COL_XS, COL_BC, COL_Q, COL_QI, COL_CKV, COL_G = 0, 2048, 4096, 5120, 6144, 6656, 7168
N_BIG = 9216
N_SMALL = 256
SM_DT, SM_W = 0, 32


def _nt_dot(a, b):
    return lax.dot_general(a, b, (((1,), (1,)), ((), ())), preferred_element_type=F32)


def _dot(a, b):
    return jnp.dot(a, b, preferred_element_type=F32)


def _split3(a):
    hi = a.astype(BF16)
    r = a - hi.astype(F32)
    mid = r.astype(BF16)
    lo = (r - mid.astype(F32)).astype(BF16)
    return hi, mid, lo


def _silu(v):
    hv = 0.5 * v
    return hv + hv * jnp.tanh(hv)


def _in_proj_kernel(x_ref, g_ref, wb_ref, ws_ref, ub_ref, us_ref, h_ref):
    j = pl.program_id(1)

    @pl.when(j == 0)
    def _():
        xf = x_ref[...]
        y = xf * lax.rsqrt(jnp.mean(xf * xf, axis=-1, keepdims=True) + NORM_EPS)
        h = (y * g_ref[...]).astype(BF16)
        h_ref[...] = h
        us_ref[...] = _dot(h, ws_ref[...])

    ub_ref[...] = _dot(h_ref[...], wb_ref[...]).astype(BF16)


def _in_proj(x2, g, w_big, w_small, tm, tn):
    t, d = x2.shape
    return pl.pallas_call(
        _in_proj_kernel,
        grid=(t // tm, N_BIG // tn),
        in_specs=[
            pl.BlockSpec((tm, d), lambda i, j: (i, 0)),
            pl.BlockSpec((1, d), lambda i, j: (0, 0)),
            pl.BlockSpec((d, tn), lambda i, j: (0, j)),
            pl.BlockSpec((d, N_SMALL), lambda i, j: (0, 0)),
        ],
        out_specs=[
            pl.BlockSpec((tm, tn), lambda i, j: (i, j)),
            pl.BlockSpec((tm, N_SMALL), lambda i, j: (i, 0)),
        ],
        out_shape=[
            jax.ShapeDtypeStruct((t, N_BIG), BF16),
            jax.ShapeDtypeStruct((t, N_SMALL), F32),
        ],
        scratch_shapes=[pltpu.VMEM((tm, d), BF16)],
        compiler_params=pltpu.CompilerParams(
            dimension_semantics=("arbitrary", "arbitrary"), vmem_limit_bytes=VMEM_LIMIT),
        name="in_proj",
    )(x2, g, w_big, w_small)


def _ssd_kernel(z_ref, xs_ref, bc_ref, sm_ref, cwx_ref, cwb_ref, cbx_ref, cbb_ref,
                dtb_ref, alog_ref, dsk_ref, ng_ref, e3_ref, tri_ref, it_ref, cm_ref, bm_ref, shift_ref,
                y_ref, xpx_ref, xpb_ref, st_ref, *, nch):
    lb = nch * CHUNK
    d_in = xs_ref.shape[-1]
    heads = d_in // SSM_HEAD_DIM
    gw = d_in // SSM_GROUPS
    gn = SSM_STATE

    @pl.when(pl.program_id(1) == 0)
    def _():
        xpx_ref[0:CONV_PAD, :] = jnp.zeros((CONV_PAD, d_in), BF16)
        xpb_ref[0:CONV_PAD, :] = jnp.zeros((CONV_PAD, xpb_ref.shape[-1]), BF16)
        st_ref[...] = jnp.zeros(st_ref.shape, F32)

    def conv(src_ref, pad_ref, w_ref, b_ref):
        cur = src_ref[...]
        pad_ref[CONV_PAD:CONV_PAD + lb, :] = cur
        delayed = _dot(shift_ref[...], pad_ref[...])
        acc = b_ref[...] + w_ref[SSM_CONV - 1:SSM_CONV, :] * cur.astype(F32)
        for k in range(SSM_CONV - 1):
            acc = acc + w_ref[k:k + 1, :] * delayed[k * lb:(k + 1) * lb, :]
        pad_ref[0:CONV_PAD, :] = pad_ref[lb:lb + CONV_PAD, :]
        return _silu(acc)

    xs = conv(xs_ref, xpx_ref, cwx_ref, cbx_ref)
    bc = conv(bc_ref, xpb_ref, cwb_ref, cbb_ref)

    dt_in = sm_ref[:, SM_DT:SM_DT + heads] + dtb_ref[...]
    dt = jnp.maximum(dt_in, 0.0) + jnp.log1p(jnp.exp(-jnp.abs(dt_in)))
    da = dt * (-jnp.exp(alog_ref[...]))
    tri = tri_ref[...]
    hi, mid, lo = _split3(da)
    a_cs = _dot(tri, hi) + _dot(tri, mid) + _dot(tri, lo)
    a_last = jnp.concatenate(
        [jnp.broadcast_to(a_cs[c * CHUNK + CHUNK - 1:c * CHUNK + CHUNK, :], (CHUNK, heads))
         for c in range(nch)], axis=0)
    stacked = jnp.concatenate([a_cs, jnp.exp(a_cs), jnp.exp(a_last - a_cs), dt], axis=0)
    s_hi, s_mid, s_lo = _split3(stacked)
    ex = _dot(jnp.concatenate([s_hi, s_mid, s_lo], axis=1), e3_ref[...])

    it_mask = it_ref[...] > 0.0
    cm_mask = cm_ref[...] > 0.0
    bm = bm_ref[...]
    it_bf = it_ref[...].astype(BF16)
    zf = z_ref[...].astype(F32)
    zg = _silu(zf)

    for c in range(nch):
        r0 = c * CHUNK
        a_l = ex[r0:r0 + CHUNK]
        exp_l = ex[lb + r0:lb + r0 + CHUNK]
        dst = ex[2 * lb + r0:2 * lb + r0 + CHUNK]
        dtx = ex[3 * lb + r0:3 * lb + r0 + CHUNK]
        xs_c = xs[r0:r0 + CHUNK]
        xdt = xs_c * dtx
        a_s = jnp.sum(jnp.where(it_mask, a_l, 0.0), axis=0, keepdims=True)
        decay = jnp.exp(jnp.where(cm_mask, a_l - a_s, -jnp.inf))
        xst = (xdt * dst).astype(BF16)
        xdt_bf = xdt.astype(BF16)
        y_parts = []
        for g in range(SSM_GROUPS):
            b_g = bc[r0:r0 + CHUNK, g * gn:(g + 1) * gn]
            c_g = bc[r0:r0 + CHUNK, (SSM_GROUPS + g) * gn:(SSM_GROUPS + g + 1) * gn]
            b_bf = b_g.astype(BF16)
            c_bf = c_g.astype(BF16)
            cb = _nt_dot(c_bf, b_bf)
            cbx = _dot(cb.astype(BF16), it_bf[:, 0:gw])
            w = (cbx * decay[:, g * gw:(g + 1) * gw]).astype(BF16)
            yd = []
            for jt in range(gw // 256):
                lo_c = g * gw + jt * 256
                xt = xdt_bf[:, lo_c:lo_c + 256]
                bd = jnp.concatenate([xt] * 4, axis=0) * bm
                yd.append(_dot(w[:, jt * 256:(jt + 1) * 256], bd))
            y_diag = jnp.concatenate(yd, axis=1)
            st_g = st_ref[g]
            y_off = _dot(c_bf, st_g.astype(BF16)) * exp_l[:, g * gw:(g + 1) * gw]
            new_st = _dot(jnp.transpose(b_g).astype(BF16), xst[:, g * gw:(g + 1) * gw])
            st_ref[g] = st_g * exp_l[CHUNK - 1:CHUNK, g * gw:(g + 1) * gw] + new_st
            y_g = y_diag + y_off + dsk_ref[:, g * gw:(g + 1) * gw] * xs_c[:, g * gw:(g + 1) * gw]
            y_g = y_g * zg[r0:r0 + CHUNK, g * gw:(g + 1) * gw]
            y_g = y_g * lax.rsqrt(jnp.mean(y_g * y_g, axis=-1, keepdims=True) + NORM_EPS)
            y_parts.append(y_g * ng_ref[:, g * gw:(g + 1) * gw])
        y_ref[r0:r0 + CHUNK, :] = jnp.concatenate(y_parts, axis=1).astype(BF16)


def _ssd(u_big, u_small, conv_w, conv_b, dt_bias, a_log, d_skip, ssm_norm_g, b, s, nch):
    t = b * s
    d_in = COL_BC - COL_XS
    n_bc = COL_Q - COL_BC
    heads = d_in // SSM_HEAD_DIM
    lb = nch * CHUNK
    nblk = s // lb
    assert d_in // SSM_GROUPS == 512 and SSM_HEAD_DIM == CHUNK

    e = np.kron(np.eye(heads, dtype=np.float32), np.ones((1, SSM_HEAD_DIM), np.float32))
    e3 = jnp.asarray(np.concatenate([e, e, e], axis=0), BF16)
    tri = jnp.asarray(np.kron(np.eye(nch, dtype=np.float32), np.tril(np.ones((CHUNK, CHUNK), np.float32))), BF16)
    it = jnp.asarray(np.tile(np.eye(CHUNK, dtype=np.float32), (1, heads)))
    cm = jnp.asarray(np.tile(np.tril(np.ones((CHUNK, CHUNK), np.float32)), (1, heads)))
    bm = jnp.asarray(np.kron(np.eye(4, dtype=np.float32), np.ones((64, 64), np.float32)), BF16)
    shift = np.zeros(((SSM_CONV - 1) * lb, CONV_PAD + lb), np.float32)
    for k in range(SSM_CONV - 1):
        shift[k * lb + np.arange(lb), CONV_PAD + np.arange(lb) - (SSM_CONV - 1 - k)] = 1.0
    shift = jnp.asarray(shift, BF16)

    cwx, cwb = conv_w[:, :d_in], conv_w[:, d_in:]
    cbx, cbb = conv_b[None, :d_in], conv_b[None, d_in:]
    dsk = jnp.repeat(d_skip, SSM_HEAD_DIM)[None, :]
    const = lambda shape: pl.BlockSpec(shape, lambda bi, ci: (0,) * len(shape))
    row = lambda width, col: pl.BlockSpec((lb, width), lambda bi, ci: (bi * nblk + ci, col))

    return pl.pallas_call(
        functools.partial(_ssd_kernel, nch=nch),
        grid=(b, nblk),
        in_specs=[
            row(d_in, COL_Z // d_in), row(d_in, COL_XS // d_in), row(n_bc, COL_BC // n_bc),
            row(LANES, 1),
            const((SSM_CONV, d_in)), const((SSM_CONV, n_bc)), const((1, d_in)), const((1, n_bc)),
            const((1, heads)), const((1, heads)), const((1, d_in)), const((1, d_in)),
            const((3 * heads, d_in)), const((lb, lb)), const((CHUNK, d_in)), const((CHUNK, d_in)),
            const((256, 256)), const(((SSM_CONV - 1) * lb, CONV_PAD + lb)),
        ],
        out_specs=pl.BlockSpec((lb, d_in), lambda bi, ci: (bi * nblk + ci, 0)),
        out_shape=jax.ShapeDtypeStruct((t, d_in), BF16),
        scratch_shapes=[
            pltpu.VMEM((lb + CONV_PAD, d_in), BF16),
            pltpu.VMEM((lb + CONV_PAD, n_bc), BF16),
            pltpu.VMEM((SSM_GROUPS, SSM_STATE, d_in // SSM_GROUPS), F32),
        ],
        compiler_params=pltpu.CompilerParams(
            dimension_semantics=("arbitrary", "arbitrary"), vmem_limit_bytes=VMEM_LIMIT),
        name="ssd",
    )(u_big, u_big, u_big, u_small, cwx, cwb, cbx, cbb, dt_bias[None, :], a_log[None, :], dsk,
      ssm_norm_g[None, :], e3, tri, it, cm, bm, shift)


def _kv_proj_kernel(c_ref, g_ref, wk_ref, wvt_ref, k_ref, vt_ref):
    cf = c_ref[...].astype(F32)
    y = cf * lax.rsqrt(jnp.mean(cf * cf, axis=-1, keepdims=True) + NORM_EPS)
    cn = (y * g_ref[...]).astype(BF16)
    k_ref[0] = (_dot(cn, wk_ref[...]) * LOG2E).astype(BF16)
    vt = _nt_dot(wvt_ref[...], cn).astype(BF16)
    for pair in range(ATT_HEADS // 2):
        vt_ref[0, pair, 0:LANES, :] = vt[pair * LANES:(pair + 1) * LANES, :]
        vt_ref[0, pair, LANES:, :] = jnp.ones((VT_ONES, vt.shape[1]), BF16)


def _kv_proj(u_big, kv_norm_g, w_k, w_vt, b, s):
    width = ATT_HEADS * ATT_HEAD_DIM
    return pl.pallas_call(
        _kv_proj_kernel,
        grid=(b,),
        in_specs=[
            pl.BlockSpec((s, KV_LATENT), lambda bi: (bi, COL_CKV // KV_LATENT)),
            pl.BlockSpec((1, KV_LATENT), lambda bi: (0, 0)),
            pl.BlockSpec((KV_LATENT, width), lambda bi: (0, 0)),
            pl.BlockSpec((width, KV_LATENT), lambda bi: (0, 0)),
        ],
        out_specs=[
            pl.BlockSpec((1, s, width), lambda bi: (bi, 0, 0)),
            pl.BlockSpec((1, ATT_HEADS // 2, LANES + VT_ONES, s), lambda bi: (bi, 0, 0, 0)),
        ],
        out_shape=[
            jax.ShapeDtypeStruct((b, s, width), BF16),
            jax.ShapeDtypeStruct((b, ATT_HEADS // 2, LANES + VT_ONES, s), BF16),
        ],
        compiler_params=pltpu.CompilerParams(
            dimension_semantics=("arbitrary",), vmem_limit_bytes=VMEM_LIMIT),
        name="kv_proj",
    )(u_big, kv_norm_g, w_k, w_vt)


def _col_reduce(x, op):
    n = x.shape[0]
    slab = op(x.reshape(n // ROW_SLAB, ROW_SLAB, x.shape[1]), axis=0)
    return op(slab, axis=0, keepdims=True)


def _dsa_kernel(q_ref, qi_ref, smq_ref, smk_ref, k_ref, vt_ref, slope_ref, o_ref,
                kidx_ref, sc_ref, thr_ref, jcut_ref, wt_ref, q2_ref, qi2_ref, m_ref, l_ref, acc_ref, ot_ref,
                att_ref, p_ref, alpha_ref, u_ref, cu_ref, *, n_buckets, topk):
    i = pl.program_id(1)

    @pl.when(i == 0)
    def _():
        kidx_ref[...] = smk_ref[...].astype(BF16)

    lane = lax.broadcasted_iota(jnp.int32, (Q_TILE, LANES), 1)
    low_half = lane < IDX_HEAD_DIM

    def head_pair_rows(x):
        zero = jnp.zeros_like(x)
        return jnp.concatenate([jnp.where(low_half, x, zero), jnp.where(low_half, zero, x)], axis=0)

    wt_ref[...] = jnp.transpose(smq_ref[...])[SM_W:SM_W + IDX_HEADS, :] * (IDX_HEADS ** -0.5 * IDX_HEAD_DIM ** -0.5)
    for pair in range(IDX_HEADS // 2):
        qi2_ref[pair] = head_pair_rows(qi_ref[:, pair * LANES:(pair + 1) * LANES])
    for pair in range(ATT_HEADS // 2):
        q2_ref[pair] = head_pair_rows(
            q_ref[:, pair * LANES:(pair + 1) * LANES] * jnp.asarray(ATT_HEAD_DIM ** -0.5, BF16))
    kf = float(topk)

    n_chunks = ((i + 1) * Q_TILE + KEY_CHUNK - 1) // KEY_CHUNK
    row_iota = lax.broadcasted_iota(jnp.int32, (KEY_CHUNK, Q_TILE), 0)
    t_pos = i * Q_TILE + lax.broadcasted_iota(jnp.int32, (KEY_CHUNK, Q_TILE), 1)
    vis_end = (t_pos // CHUNK + 1) * CHUNK

    def chunk_rows(c):
        return pl.ds(pl.multiple_of(c * KEY_CHUNK, KEY_CHUNK), KEY_CHUNK)

    def index_chunk(c, carry):
        rows = chunk_rows(c)
        score = jnp.zeros((KEY_CHUNK, Q_TILE), F32)
        for pair in range(IDX_HEADS // 2):
            logit = _nt_dot(kidx_ref[rows, :], qi2_ref[pair])
            score = (score + jnp.maximum(logit[:, 0:Q_TILE], 0.0) * wt_ref[2 * pair:2 * pair + 1, :]
                     + jnp.maximum(logit[:, Q_TILE:], 0.0) * wt_ref[2 * pair + 1:2 * pair + 2, :])
        sc_ref[rows, :] = jnp.where(c * KEY_CHUNK + row_iota < vis_end, score, -jnp.inf)
        return carry

    lax.fori_loop(0, n_chunks, index_chunk, 0)

    many = vis_end[0:1, :] > topk

    def select(n):
        s_pos = lax.broadcasted_iota(jnp.int32, (n, Q_TILE), 0)

        def count(pred):
            return _col_reduce(jnp.where(pred, 1.0, 0.0), jnp.sum)

        u_ref[...] = jnp.zeros((1, Q_TILE), jnp.int32)
        cu_ref[...] = jnp.full((1, Q_TILE), float(n), F32)

        def bit_step(it, carry):
            u = u_ref[...]
            cand = u | (jnp.int32(1) << (31 - it))
            cand_f = pltpu.bitcast(jnp.where(cand < 0, cand ^ INT_MIN, ~cand), F32)
            c = count(sc_ref[0:n, :] >= cand_f)
            keep = c >= kf
            u_ref[...] = jnp.where(keep, cand, u)
            cu_ref[...] = jnp.where(keep, c, cu_ref[...])
            return carry

        def search(lo, hi):
            lax.fori_loop(lo, hi, bit_step, 0)

        search(0, BIT_SEARCH_FIRST)
        for lo in range(BIT_SEARCH_FIRST, 32, BIT_SEARCH_STEP):
            unresolved = jnp.max(jnp.where(many & (cu_ref[...] != kf), 1.0, 0.0)) > 0.0
            pl.when(unresolved)(functools.partial(search, lo, lo + BIT_SEARCH_STEP))
        u_thr = u_ref[...]
        thr = pltpu.bitcast(jnp.where(u_thr < 0, u_thr ^ INT_MIN, ~u_thr), F32)
        thr = jnp.where(many, thr, -jnp.inf)
        thr_ref[...] = thr
        sc = sc_ref[0:n, :]
        gt = sc > thr
        eq = sc == thr
        need = kf - count(gt)
        excess = jnp.where(many, count(eq) - need, 0.0)
        jcut_ref[...] = jnp.where(many, n, -1)

        @pl.when(jnp.max(excess) > 0.0)
        def _():
            nbits = max(1, int(np.ceil(np.log2(n))))

            def idx_step(it, j0):
                cand = j0 | (jnp.int32(1) << (nbits - 1 - it))
                c = count((sc_ref[0:n, :] == thr) & (s_pos < cand))
                return jnp.where(c < need, cand, j0)

            j0 = lax.fori_loop(0, nbits, idx_step, jnp.zeros((1, Q_TILE), jnp.int32))
            jcut_ref[...] = jnp.where(many, j0, -1)

    for nb in range(1, n_buckets + 1):
        pl.when(n_chunks == nb)(functools.partial(select, nb * KEY_CHUNK))

    m_ref[...] = jnp.full(m_ref.shape, SOFTMAX_M0, F32)
    l_ref[...] = jnp.zeros(l_ref.shape, F32)
    acc_ref[...] = jnp.zeros(acc_ref.shape, F32)
    hd = ATT_HEAD_DIM

    def attend_chunk(c, carry):
        rows = chunk_rows(c)
        s_pos = c * KEY_CHUNK + row_iota
        sc = sc_ref[rows, :]
        sel = (sc > thr_ref[...]) | ((sc == thr_ref[...]) & (s_pos <= jcut_ref[...]))
        dsel = jnp.where(sel, jnp.abs(t_pos - s_pos).astype(F32), jnp.inf)
        for pair in range(ATT_HEADS // 2):
            att_ref[pair] = _nt_dot(k_ref[0, rows, pair * LANES:(pair + 1) * LANES], q2_ref[pair])
        for h in range(ATT_HEADS):
            pair, half = divmod(h, 2)
            cols = slice(half * Q_TILE, (half + 1) * Q_TILE)
            a = att_ref[pair, :, cols] - slope_ref[h:h + 1, :] * dsel
            m_old = m_ref[h:h + 1, :]
            m_new = jnp.maximum(m_old, _col_reduce(a, jnp.max))
            m_ref[h:h + 1, :] = m_new
            alpha_ref[h:h + 1, :] = jnp.exp2(m_old - m_new)
            p_ref[pair, :, cols] = jnp.exp2(a - m_new).astype(BF16)
        for pair in range(ATT_HEADS // 2):
            o2 = _dot(vt_ref[0, pair, :, rows], p_ref[pair])
            for half in range(2):
                h = 2 * pair + half
                cols = slice(half * Q_TILE, (half + 1) * Q_TILE)
                alpha = alpha_ref[h:h + 1, :]
                acc_ref[h] = acc_ref[h] * alpha + o2[half * hd:(half + 1) * hd, cols]
                l_ref[h:h + 1, :] = l_ref[h:h + 1, :] * alpha + o2[2 * hd:2 * hd + 1, cols]
        return carry

    lax.fori_loop(0, n_chunks, attend_chunk, 0)
    for h in range(ATT_HEADS):
        ot_ref[h * hd:(h + 1) * hd, :] = acc_ref[h] * (1.0 / l_ref[h:h + 1, :])
    o_ref[...] = jnp.transpose(ot_ref[...]).astype(BF16)


def _dsa(u_big, u_small, k_all, vt_all, b, s):
    t = b * s
    width = ATT_HEADS * ATT_HEAD_DIM
    nq = s // Q_TILE
    topk = min(TOPK_MAX, s // 4)
    assert s % KEY_CHUNK == 0
    slopes = 2.0 ** (-8.0 * jnp.arange(1, ATT_HEADS + 1, dtype=F32) / ATT_HEADS) * LOG2E
    slopes = jnp.broadcast_to(slopes[:, None], (ATT_HEADS, Q_TILE))
    qi_w = IDX_HEADS * IDX_HEAD_DIM
    return pl.pallas_call(
        functools.partial(_dsa_kernel, n_buckets=s // KEY_CHUNK, topk=topk),
        grid=(b, nq),
        in_specs=[
            pl.BlockSpec((Q_TILE, width), lambda bi, qi: (bi * nq + qi, COL_Q // width)),
            pl.BlockSpec((Q_TILE, qi_w), lambda bi, qi: (bi * nq + qi, COL_QI // qi_w)),
            pl.BlockSpec((Q_TILE, LANES), lambda bi, qi: (bi * nq + qi, 1)),
            pl.BlockSpec((s, LANES), lambda bi, qi: (bi, 0)),
            pl.BlockSpec((1, s, width), lambda bi, qi: (bi, 0, 0)),
            pl.BlockSpec((1, ATT_HEADS // 2, LANES + VT_ONES, s), lambda bi, qi: (bi, 0, 0, 0)),
            pl.BlockSpec((ATT_HEADS, Q_TILE), lambda bi, qi: (0, 0)),
        ],
        out_specs=pl.BlockSpec((Q_TILE, width), lambda bi, qi: (bi * nq + qi, 0)),
        out_shape=jax.ShapeDtypeStruct((t, width), BF16),
        scratch_shapes=[
            pltpu.VMEM((s, LANES), BF16),
            pltpu.VMEM((s, Q_TILE), F32),
            pltpu.VMEM((1, Q_TILE), F32),
            pltpu.VMEM((1, Q_TILE), jnp.int32),
            pltpu.VMEM((IDX_HEADS, Q_TILE), F32),
            pltpu.VMEM((ATT_HEADS // 2, 2 * Q_TILE, LANES), BF16),
            pltpu.VMEM((IDX_HEADS // 2, 2 * Q_TILE, LANES), BF16),
            pltpu.VMEM((ATT_HEADS, Q_TILE), F32),
            pltpu.VMEM((ATT_HEADS, Q_TILE), F32),
            pltpu.VMEM((ATT_HEADS, ATT_HEAD_DIM, Q_TILE), F32),
            pltpu.VMEM((width, Q_TILE), F32),
            pltpu.VMEM((ATT_HEADS // 2, KEY_CHUNK, 2 * Q_TILE), F32),
            pltpu.VMEM((ATT_HEADS // 2, KEY_CHUNK, 2 * Q_TILE), BF16),
            pltpu.VMEM((ATT_HEADS, Q_TILE), F32),
            pltpu.VMEM((1, Q_TILE), jnp.int32),
            pltpu.VMEM((1, Q_TILE), F32),
        ],
        compiler_params=pltpu.CompilerParams(
            dimension_semantics=("arbitrary", "arbitrary"), vmem_limit_bytes=VMEM_LIMIT),
        name="dsa",
    )(u_big, u_big, u_small, u_small, k_all, vt_all, slopes)


def _merge_kernel(x_ref, ys_ref, ya_ref, g0_ref, g1_ref, gb_ref, wbs_ref, wba_ref, wo_ref,
                  nf_ref, wr_ref, br_ref, x1_ref, h2_ref, gate_ref, gselc_ref, gselr_ref):
    def sig(v):
        return 0.5 + 0.5 * jnp.tanh(0.5 * v)

    g0 = sig(g0_ref[...].astype(F32) + gb_ref[0:1, :])
    g1 = sig(g1_ref[...].astype(F32) + gb_ref[1:2, :])
    mixed = g0 * _dot(ys_ref[...], wbs_ref[...]) + g1 * _dot(ya_ref[...], wba_ref[...])
    x1 = x_ref[...] + _dot(mixed.astype(BF16), wo_ref[...])
    x1_ref[...] = x1
    h2 = x1 * lax.rsqrt(jnp.mean(x1 * x1, axis=-1, keepdims=True) + NORM_EPS) * nf_ref[...]
    h2b = h2.astype(BF16)
    h2_ref[...] = h2b

    n_e = MOE_GROUPS * MOE_EXPERTS
    logits = _dot(h2b, wr_ref[...]) + br_ref[...]
    lane = lax.broadcasted_iota(jnp.int32, logits.shape, 1)
    neg = -jnp.inf
    is_g = (lane >= n_e) & (lane < n_e + MOE_GROUPS)
    gl = jnp.where(is_g, logits, neg)
    gmax = jnp.max(gl, axis=-1, keepdims=True)
    g_val = 1.0 / jnp.sum(jnp.exp(gl - gmax), axis=-1, keepdims=True)
    g_sel = jnp.min(jnp.where(is_g & (gl == gmax), lane, 4 * LANES), axis=-1, keepdims=True) - n_e
    in_grp = (lane < n_e) & ((lane // MOE_EXPERTS) == g_sel)
    el = jnp.where(in_grp, logits, neg)
    m1 = jnp.max(el, axis=-1, keepdims=True)
    i1 = jnp.min(jnp.where(in_grp & (el == m1), lane, 4 * LANES), axis=-1, keepdims=True)
    el2 = jnp.where(lane == i1, neg, el)
    m2 = jnp.max(el2, axis=-1, keepdims=True)
    i2 = jnp.min(jnp.where(in_grp & (el2 == m2), lane, 4 * LANES), axis=-1, keepdims=True)
    p2 = jnp.exp(m2 - m1)
    inv = 1.0 / (1.0 + p2)
    gate = (jnp.where(lane == i1, inv, 0.0) + jnp.where(lane == i2, p2 * inv, 0.0)) * g_val
    gate8 = gate
    for g in range(1, MOE_GROUPS):
        gate8 = gate8 + pltpu.roll(gate, LANES - g * MOE_EXPERTS, 1)
    gate_ref[...] = jnp.where(lane < MOE_EXPERTS, gate8, 0.0)
    gsel = jnp.broadcast_to(g_sel.astype(F32), logits.shape)
    gselc_ref[...] = gsel
    gselr_ref[...] = jnp.transpose(gsel)[0:8, :]


def _merge(x2, y_ssm, y_att, u_big, gate_b, w_bs, w_ba, w_o, norm_ffn_g, w_r, b_r, tm):
    t, d = x2.shape
    d_in = y_ssm.shape[1]
    const = lambda shape: pl.BlockSpec(shape, lambda i: (0,) * len(shape))
    return pl.pallas_call(
        _merge_kernel,
        grid=(t // tm,),
        in_specs=[
            pl.BlockSpec((tm, d), lambda i: (i, 0)),
            pl.BlockSpec((tm, d_in), lambda i: (i, 0)),
            pl.BlockSpec((tm, d), lambda i: (i, 0)),
            pl.BlockSpec((tm, d), lambda i: (i, COL_G // d)),
            pl.BlockSpec((tm, d), lambda i: (i, COL_G // d + 1)),
            const((2, d)), const((d_in, d)), const((d, d)), const((d, d)), const((1, d)),
            const((d, LANES)), const((1, LANES)),
        ],
        out_specs=[
            pl.BlockSpec((tm, d), lambda i: (i, 0)),
            pl.BlockSpec((tm, d), lambda i: (i, 0)),
            pl.BlockSpec((tm, LANES), lambda i: (i, 0)),
            pl.BlockSpec((tm, LANES), lambda i: (i, 0)),
            pl.BlockSpec((8, tm), lambda i: (0, i)),
        ],
        out_shape=[
            jax.ShapeDtypeStruct((t, d), F32),
            jax.ShapeDtypeStruct((t, d), BF16),
            jax.ShapeDtypeStruct((t, LANES), F32),
            jax.ShapeDtypeStruct((t, LANES), F32),
            jax.ShapeDtypeStruct((8, t), F32),
        ],
        compiler_params=pltpu.CompilerParams(
            dimension_semantics=("arbitrary",), vmem_limit_bytes=VMEM_LIMIT),
        name="merge",
    )(x2, y_ssm, y_att, u_big, u_big, gate_b, w_bs, w_ba, w_o, norm_ffn_g, w_r, b_r)


MOE_SUB = 256
MOE_ROWS = 128
MOE_HALVES = 2


def _moe_kernel(h_ref, gate_ref, gselc_ref, gselr_ref, x1_ref, w1_ref, w3_ref, w2_ref, e_ref,
                triu_ref, tril_ref, nf_ref, o_ref, xc_ref, gc_ref, yc_ref, off_ref, cnt_ref):
    i, g, half = pl.program_id(0), pl.program_id(1), pl.program_id(2)
    tt = h_ref.shape[0]
    n_sub = tt // MOE_SUB
    g_f = g.astype(F32)

    @pl.when((i == 0) & (g == 0) & (half == 0))
    def _():
        xc_ref[...] = jnp.zeros(xc_ref.shape, BF16)
        gc_ref[...] = jnp.zeros(gc_ref.shape, F32)
        yc_ref[...] = jnp.zeros(yc_ref.shape, F32)

    @pl.when((g == 0) & (half == 0))
    def _():
        o_ref[...] = x1_ref[...]

    @pl.when(half == 0)
    def _():
        off = jnp.int32(0)
        for s in range(n_sub):
            tok = slice(s * MOE_SUB, (s + 1) * MOE_SUB)
            member = gselr_ref[:, tok] == g_f
            m_f = jnp.where(member, 1.0, 0.0)
            rank = _dot(m_f.astype(BF16), triu_ref[...])
            cnt = jnp.sum(m_f[0:1, :]).astype(jnp.int32)
            off_ref[s] = off
            cnt_ref[s] = cnt
            g8 = gate_ref[tok, :]
            g_hi = g8.astype(BF16)
            g_lo = (g8 - g_hi.astype(F32)).astype(BF16)

            rank = jnp.where(member, rank, -1.0)

            def compact(rbase, off=off, tok=tok, rank=rank, g_hi=g_hi, g_lo=g_lo):
                r = (rbase + lax.broadcasted_iota(jnp.int32, (MOE_ROWS, MOE_SUB), 0)).astype(F32)
                onehot = jnp.where(r == rank[0:1, :], 1.0, 0.0).astype(BF16)
                rows16 = pl.ds(pl.multiple_of(off + rbase, 16), MOE_ROWS)
                xc_ref[rows16, :] = _dot(onehot, h_ref[tok, :]).astype(BF16)
                gc_ref[rows16, :] = _dot(onehot, g_hi) + _dot(onehot, g_lo)

            compact(0)
            pl.when(cnt > MOE_ROWS)(functools.partial(compact, MOE_ROWS))
            off = off + ((cnt + 15) // 16) * 16
        off_ref[n_sub] = off

    def ffn_block(blk, carry):
        rows = pl.ds(pl.multiple_of(blk * MOE_ROWS, MOE_ROWS), MOE_ROWS)
        x = xc_ref[rows, :]
        gx = gc_ref[rows, :]
        gx_hi = gx.astype(BF16)
        gx_lo = (gx - gx_hi.astype(F32)).astype(BF16)
        gexp = _dot(gx_hi, e_ref[0]) + _dot(gx_lo, e_ref[0])
        eh, ff = w1_ref.shape[1], w1_ref.shape[-1]
        a = jnp.concatenate([_dot(x, w1_ref[0, e]) for e in range(eh)], axis=1)
        b3 = jnp.concatenate([_dot(x, w3_ref[0, e]) for e in range(eh)], axis=1)
        hid = (_silu(a) * b3 * gexp).astype(BF16)
        y = _dot(hid, w2_ref[0].reshape(eh * ff, w2_ref.shape[-1]))

        @pl.when(half == 0)
        def _():
            yc_ref[rows, :] = y

        @pl.when(half != 0)
        def _():
            yc_ref[rows, :] += y

        return carry

    lax.fori_loop(0, (off_ref[n_sub] + MOE_ROWS - 1) // MOE_ROWS, ffn_block, 0)

    @pl.when(half == MOE_HALVES - 1)
    def _():
        for s in range(n_sub):
            tok = slice(s * MOE_SUB, (s + 1) * MOE_SUB)
            member = gselc_ref[tok, :] == g_f
            rank = _dot(tril_ref[...], jnp.where(member, 1.0, 0.0).astype(BF16))
            rank = jnp.where(member, rank, -1.0)
            off = off_ref[s]

            def scatter(rbase, off=off, tok=tok, rank=rank):
                r = (rbase + lax.broadcasted_iota(jnp.int32, (MOE_SUB, MOE_ROWS), 1)).astype(F32)
                onehot_t = jnp.where(r == rank, 1.0, 0.0).astype(BF16)
                rows8 = pl.ds(pl.multiple_of(off + rbase, 8), MOE_ROWS)
                o_ref[tok, :] += _dot(onehot_t, yc_ref[rows8, :].astype(BF16))

            scatter(0)
            pl.when(cnt_ref[s] > MOE_ROWS)(functools.partial(scatter, MOE_ROWS))

    @pl.when((g == pl.num_programs(1) - 1) & (half == MOE_HALVES - 1))
    def _():
        x2 = o_ref[...]
        o_ref[...] = x2 * lax.rsqrt(jnp.mean(x2 * x2, axis=-1, keepdims=True) + NORM_EPS) * nf_ref[...]


def _moe(h2, gate8, gselc, gselr, x1, w1, w3, w2, norm_final_g, tt):
    t, d = x1.shape
    n_g, n_e, _, ff = w1.shape
    eh = n_e // MOE_HALVES
    assert MOE_ROWS == LANES and tt % MOE_SUB == 0 and n_e <= 8
    expand = np.zeros((MOE_HALVES, LANES, eh * ff), np.float32)
    for hf in range(MOE_HALVES):
        for e in range(eh):
            expand[hf, hf * eh + e, e * ff:(e + 1) * ff] = 1.0
    triu = np.triu(np.ones((MOE_SUB, MOE_SUB), np.float32), 1)
    xrows = tt + 2 * MOE_ROWS
    const = lambda shape: pl.BlockSpec(shape, lambda i, g, hf: (0,) * len(shape))
    tok = lambda width: pl.BlockSpec((tt, width), lambda i, g, hf: (i, 0))
    wspec = pl.BlockSpec((1, eh, d, ff), lambda i, g, hf: (g, hf, 0, 0))
    return pl.pallas_call(
        _moe_kernel,
        grid=(t // tt, n_g, MOE_HALVES),
        in_specs=[
            tok(d), tok(LANES), tok(LANES),
            pl.BlockSpec((8, tt), lambda i, g, hf: (0, i)),
            tok(d),
            wspec, wspec,
            pl.BlockSpec((1, eh, ff, d), lambda i, g, hf: (g, hf, 0, 0)),
            pl.BlockSpec((1, LANES, eh * ff), lambda i, g, hf: (hf, 0, 0)),
            const((MOE_SUB, MOE_SUB)), const((MOE_SUB, MOE_SUB)), const((1, d)),
        ],
        out_specs=pl.BlockSpec((tt, d), lambda i, g, hf: (i, 0)),
        out_shape=jax.ShapeDtypeStruct((t, d), F32),
        scratch_shapes=[
            pltpu.VMEM((xrows, d), BF16),
            pltpu.VMEM((xrows, LANES), F32),
            pltpu.VMEM((xrows, d), F32),
            pltpu.SMEM((8,), jnp.int32),
            pltpu.SMEM((8,), jnp.int32),
        ],
        compiler_params=pltpu.CompilerParams(
            dimension_semantics=("arbitrary", "arbitrary", "arbitrary"), vmem_limit_bytes=VMEM_LIMIT),
        name="moe",
    )(h2, gate8, gselc, gselr, x1, w1, w3, w2, jnp.asarray(expand, BF16),
      jnp.asarray(triu, BF16), jnp.asarray(triu.T, BF16), norm_final_g)


def _pick_tile(t, pref):
    tm = pref
    while t % tm:
        tm //= 2
    return tm


def kernel(x, norm_mix_g, w_in, conv_w, conv_b, dt_bias, a_log, d_skip, ssm_norm_g, kv_norm_g, w_uk, w_uv,
           gate_b, w_branch_ssm, w_branch_att, w_out, norm_ffn_g, w_group, b_group, w_router, b_router,
           w_e1, w_e3, w_e2, norm_final_g):
    b, s, d = x.shape
    depth = w_in.shape[0]
    t = b * s
    d_in = 2 * d
    heads = d_in // SSM_HEAD_DIM
    n_bc = 2 * SSM_GROUPS * SSM_STATE
    width = ATT_HEADS * ATT_HEAD_DIM
    qi_w = IDX_HEADS * IDX_HEAD_DIM
    assert (d, d_in, n_bc, width) == (1024, COL_XS - COL_Z, COL_Q - COL_BC, COL_QI - COL_Q)
    assert s % (2 * CHUNK) == 0 and s % Q_TILE == 0

    assert depth == 1, "the final RMSNorm is fused into the expert kernel of the only layer"
    layer = 0
    x2 = x.reshape(t, d)

    sizes = (d_in, d_in + n_bc, heads, width, KV_LATENT, qi_w, IDX_HEAD_DIM, IDX_HEADS, 2 * d)
    cuts = np.concatenate([[0], np.cumsum(sizes)])
    seg = lambda k: w_in[layer][:, int(cuts[k]):int(cuts[k + 1])]
    w_z, w_xbc, w_dt, w_q, w_ckv, w_qi, w_ki, w_wi, w_g = (seg(k) for k in range(9))
    zpad = lambda n: jnp.zeros((d, n), w_in.dtype)
    w_big = jnp.concatenate(
        [w_z, w_xbc, w_q, w_qi, w_ckv, zpad(COL_G - COL_CKV - KV_LATENT), w_g], axis=1).astype(BF16)
    w_small = jnp.concatenate(
        [w_ki, w_ki, w_dt, w_wi, zpad(N_SMALL - 2 * IDX_HEAD_DIM - heads - IDX_HEADS)], axis=1).astype(BF16)
    assert w_big.shape[1] == N_BIG and w_small.shape[1] == N_SMALL

    u_big, u_small = _in_proj(x2, norm_mix_g[layer][None, :], w_big, w_small, _pick_tile(t, 1024), 1536)

    y_ssm = _ssd(u_big, u_small, conv_w[layer], conv_b[layer], dt_bias[layer], a_log[layer],
                 d_skip[layer], ssm_norm_g[layer], b, s, 2)

    w_k = jnp.transpose(w_uk[layer], (1, 0, 2)).reshape(KV_LATENT, width).astype(BF16)
    w_vt = jnp.transpose(w_uv[layer], (0, 2, 1)).reshape(width, KV_LATENT).astype(BF16)
    k_all, vt_all = _kv_proj(u_big, kv_norm_g[layer][None, :], w_k, w_vt, b, s)
    y_att = _dsa(u_big, u_small, k_all, vt_all, b, s)

    n_e = MOE_GROUPS * MOE_EXPERTS
    w_r = jnp.concatenate(
        [jnp.transpose(w_router[layer], (1, 0, 2)).reshape(d, n_e), w_group[layer],
         jnp.zeros((d, LANES - n_e - MOE_GROUPS), F32)], axis=1).astype(BF16)
    b_r = jnp.concatenate(
        [b_router[layer].reshape(n_e), b_group[layer], jnp.zeros((LANES - n_e - MOE_GROUPS,), F32)])[None, :]
    x1, h2, gate8, gselc, gselr = _merge(
        x2, y_ssm, y_att, u_big, gate_b[layer],
        w_branch_ssm[layer].astype(BF16), w_branch_att[layer].astype(BF16),
        w_out[layer].astype(BF16), norm_ffn_g[layer][None, :], w_r, b_r, _pick_tile(t, 512))

    out = _moe(h2, gate8, gselc, gselr, x1, w_e1[layer].astype(BF16), w_e3[layer].astype(BF16),
               w_e2[layer].astype(BF16), norm_final_g[None, :], _pick_tile(t, 1024))
    return out.reshape(b, s, d)
```
